```python
import jax, jax.numpy as jnp
from jax import lax
import numpy as np

D_MODEL = 1024
BATCH = 16
SEQ = 2048
DEPTH = 1

D_MIX = D_MODEL
C_CONV = D_MIX // 2
V_HEAD = 64
N_HEADS = (D_MIX - C_CONV) // V_HEAD
QK_NOPE = 64
QK_ROPE = 32
QK_HEAD = QK_NOPE + QK_ROPE
Q_LORA = D_MODEL // 4
KV_LORA = D_MODEL // 8
CONV_K = 31
CONV_PAD = CONV_K // 2
D_FF = 256 * ((8 * D_MODEL // 3 + 255) // 256)
N_MOD = 9
W_IN_COLS = 2 * C_CONV + Q_LORA + KV_LORA + QK_ROPE
ROPE_THETA = 10000.0
Q_BLOCK = 128
EPS = 1e-6

kernel_name = "hybrid_conv_mla_macaron_adaln_encoder"


def _rms(x):
    xf = x.astype(jnp.float32)
    return (xf * lax.rsqrt(jnp.mean(xf * xf, axis=-1, keepdims=True) + EPS)).astype(x.dtype)


def _layernorm(x, g, b):
    xf = x.astype(jnp.float32)
    mu = jnp.mean(xf, axis=-1, keepdims=True)
    var = jnp.mean(jnp.square(xf - mu), axis=-1, keepdims=True)
    return ((xf - mu) * lax.rsqrt(var + 1e-5)).astype(x.dtype) * g + b


def _modulate(x, shift, scale):
    return _rms(x) * (1 + scale[:, None, :]) + shift[:, None, :]


def _swiglu(x, w_gate, w_up, w_down):
    return (jax.nn.silu(x @ w_gate) * (x @ w_up)) @ w_down


def _rope(x, cos, sin):
    half = x.shape[-1] // 2
    x1, x2 = x[..., :half], x[..., half:]
    return jnp.concatenate([x1 * cos - x2 * sin, x2 * cos + x1 * sin], axis=-1)


def _attention(q, k, v):
    B, S, H, Dq = q.shape
    nb = S // Q_BLOCK
    qb = q.reshape(B, nb, Q_BLOCK, H, Dq).transpose(1, 0, 2, 3, 4)
    scale = Dq ** -0.5

    def one_block(qblk):
        s = jnp.einsum('bqhd,bkhd->bhqk', qblk, k, preferred_element_type=jnp.float32) * scale
        p = jax.nn.softmax(s, axis=-1).astype(v.dtype)
        return jnp.einsum('bhqk,bkhd->bqhd', p, v)

    o = lax.map(one_block, qb)
    return o.transpose(1, 0, 2, 3, 4).reshape(B, S, H * v.shape[-1])


def setup_inputs(seed: int = 0) -> dict:
    key = jax.random.key(seed)
    ks = jax.random.split(key, 24)

    def w(k, shape, fan_in):
        return jax.random.normal(k, shape, jnp.float32) * (fan_in ** -0.5)

    def gain(k, shape):
        return 1.0 + 0.01 * jax.random.normal(k, shape, jnp.float32)

    def bias(k, shape):
        return 0.01 * jax.random.normal(k, shape, jnp.float32)

    L = DEPTH
    return {
        "x": jax.random.normal(ks[0], (BATCH, SEQ, D_MODEL), jnp.float32),
        "c": jax.random.normal(ks[1], (BATCH, D_MODEL), jnp.float32),
        "positions": jnp.broadcast_to(jnp.arange(SEQ, dtype=jnp.int32), (BATCH, SEQ)),
        "w_ada": w(ks[2], (L, D_MODEL, N_MOD * D_MODEL), D_MODEL),
        "b_ada": bias(ks[3], (L, N_MOD * D_MODEL)),
        "ffn1_wg": w(ks[4], (L, D_MODEL, D_FF), D_MODEL),
        "ffn1_wu": w(ks[5], (L, D_MODEL, D_FF), D_MODEL),
        "ffn1_wd": w(ks[6], (L, D_FF, D_MODEL), D_FF),
        "w_in": w(ks[7], (L, D_MODEL, W_IN_COLS), D_MODEL),
        "g_q_lat": gain(ks[8], (L, Q_LORA)),
        "g_kv_lat": gain(ks[9], (L, KV_LORA)),
        "w_q_up": w(ks[10], (L, Q_LORA, N_HEADS * QK_HEAD), Q_LORA),
        "w_kv_up": w(ks[11], (L, KV_LORA, N_HEADS * (QK_NOPE + V_HEAD)), KV_LORA),
        "g_q_head": gain(ks[12], (L, QK_HEAD)),
        "g_k_head": gain(ks[13], (L, QK_HEAD)),
        "conv_w": w(ks[14], (L, CONV_K, 1, C_CONV), CONV_K),
        "conv_b": bias(ks[15], (L, C_CONV)),
        "g_conv_ln": gain(ks[16], (L, C_CONV)),
        "b_conv_ln": bias(ks[17], (L, C_CONV)),
        "w_out": w(ks[18], (L, D_MIX, D_MODEL), D_MIX),
        "ffn2_wg": w(ks[19], (L, D_MODEL, D_FF), D_MODEL),
        "ffn2_wu": w(ks[20], (L, D_MODEL, D_FF), D_MODEL),
        "ffn2_wd": w(ks[21], (L, D_FF, D_MODEL), D_FF),
        "g_final": gain(ks[22], (L, D_MODEL)),
    }


def reference(x, c, positions, w_ada, b_ada, ffn1_wg, ffn1_wu, ffn1_wd, w_in,
              g_q_lat, g_kv_lat, w_q_up, w_kv_up, g_q_head, g_k_head,
              conv_w, conv_b, g_conv_ln, b_conv_ln, w_out,
              ffn2_wg, ffn2_wu, ffn2_wd, g_final):
    B, S, _ = x.shape
    inv_freq = 1.0 / (ROPE_THETA ** (jnp.arange(0, QK_ROPE, 2, dtype=jnp.float32) / QK_ROPE))
    ang = positions.astype(jnp.float32)[..., None] * inv_freq
    cos = jnp.cos(ang)[:, :, None, :].astype(x.dtype)
    sin = jnp.sin(ang)[:, :, None, :].astype(x.dtype)
    sc = jax.nn.silu(c)

    h = x
    for i in range(DEPTH):
        mod = sc @ w_ada[i] + b_ada[i]
        (sh1, s1, g1, shm, smx, gm, sh2, s2, g2) = jnp.split(mod, N_MOD, axis=-1)

        h = h + 0.5 * g1[:, None, :] * _swiglu(_modulate(h, sh1, s1), ffn1_wg[i], ffn1_wu[i], ffn1_wd[i])

        n = _modulate(h, shm, smx)
        z = n @ w_in[i]
        o1 = 2 * C_CONV
        o2 = o1 + Q_LORA
        o3 = o2 + KV_LORA
        u, cq, ckv, kr = z[..., :o1], z[..., o1:o2], z[..., o2:o3], z[..., o3:]

        a = u[..., :C_CONV] * jax.nn.sigmoid(u[..., C_CONV:])
        a = lax.conv_general_dilated(a, conv_w[i], window_strides=(1,),
                                     padding=[(CONV_PAD, CONV_PAD)],
                                     dimension_numbers=('NWC', 'WIO', 'NWC'),
                                     feature_group_count=C_CONV) + conv_b[i]
        a = jax.nn.silu(_layernorm(a, g_conv_ln[i], b_conv_ln[i]))

        q = ((_rms(cq) * g_q_lat[i]) @ w_q_up[i]).reshape(B, S, N_HEADS, QK_HEAD)
        kv = ((_rms(ckv) * g_kv_lat[i]) @ w_kv_up[i]).reshape(B, S, N_HEADS, QK_NOPE + V_HEAD)
        k_nope, v = kv[..., :QK_NOPE], kv[..., QK_NOPE:]
        k_rope = jnp.broadcast_to(kr[:, :, None, :], (B, S, N_HEADS, QK_ROPE))
        k = jnp.concatenate([k_nope, k_rope], axis=-1)
        q = _rms(q) * g_q_head[i]
        k = _rms(k) * g_k_head[i]
        q = jnp.concatenate([q[..., :QK_NOPE], _rope(q[..., QK_NOPE:], cos, sin)], axis=-1)
        k = jnp.concatenate([k[..., :QK_NOPE], _rope(k[..., QK_NOPE:], cos, sin)], axis=-1)
        att = _attention(q, k, v)

        mix = jnp.concatenate([a, att], axis=-1) @ w_out[i]
        h = h + gm[:, None, :] * mix

        h = h + 0.5 * g2[:, None, :] * _swiglu(_modulate(h, sh2, s2), ffn2_wg[i], ffn2_wu[i], ffn2_wd[i])

        h = _rms(h) * g_final[i]
    return h
```

```python
import functools

import jax
import jax.numpy as jnp
from jax import lax
from jax.experimental import pallas as pl
from jax.experimental.pallas import tpu as pltpu

F32 = jnp.float32
BF16 = jnp.bfloat16

LANES = 128
EPS = 1e-6
LN_EPS = 1e-5
ROPE_THETA = 10000.0
V_HEAD = 64
QK_NOPE = 64
QK_ROPE = 32
QK_HEAD = QK_NOPE + QK_ROPE
ROPE_HALF = QK_ROPE // 2
CONV_HALO = 16

VMEM_LIMIT = 56 * 1024 * 1024


def _const_spec(shape):
    zeros = (0,) * len(shape)
    return pl.BlockSpec(shape, lambda *_: zeros, pipeline_mode=pl.Buffered(1))


def _sigmoid(x):
    return 1.0 / (1.0 + jnp.exp(-x))


def _rms(x):
    return x * lax.rsqrt(jnp.mean(x * x, axis=-1, keepdims=True) + EPS)


def _rope_table_kernel(pos_ref, inv_ref, cos_ref, sin_ref):
    ang = inv_ref[...] * pos_ref[...].astype(F32)
    cos_ref[...] = jnp.cos(ang)
    sin_ref[...] = jnp.sin(ang)


def _rope_tables(positions):
    B, S = positions.shape
    n = B * S
    blk = 4096
    inv_freq = 1.0 / (ROPE_THETA ** (jnp.arange(0, QK_ROPE, 2, dtype=F32) / QK_ROPE))
    cos_t, sin_t = pl.pallas_call(
        _rope_table_kernel,
        out_shape=(jax.ShapeDtypeStruct((ROPE_HALF, n), F32),) * 2,
        grid=(n // blk,),
        in_specs=[pl.BlockSpec((1, blk), lambda i: (0, i)),
                  pl.BlockSpec((ROPE_HALF, 1), lambda i: (0, 0))],
        out_specs=(pl.BlockSpec((ROPE_HALF, blk), lambda i: (0, i)),) * 2,
        name="rope_table",
    )(positions.reshape(1, n), inv_freq.reshape(ROPE_HALF, 1))
    cos = cos_t.T.reshape(B, S, ROPE_HALF)
    sin = sin_t.T.reshape(B, S, ROPE_HALF)
    ones = jnp.ones((B, S, QK_NOPE), F32)
    zeros_n = jnp.zeros((B, S, QK_NOPE), F32)
    zeros_p = jnp.zeros((B, S, LANES - QK_HEAD), F32)
    cos_f = jnp.concatenate([ones, cos, cos, zeros_p], axis=-1)
    sin_f = jnp.concatenate([zeros_n, sin, sin, zeros_p], axis=-1)
    return cos_f, sin_f


def _ada_kernel(c_ref, w_ref, b_ref, o_ref):
    c = c_ref[...]
    sc = (c * _sigmoid(c)).astype(BF16)
    o_ref[0] = jnp.dot(sc, w_ref[...].astype(BF16), preferred_element_type=F32) + b_ref[...]


def _ada_mod(c, w_ada, b_ada):
    B, D = c.shape
    n_mod = w_ada.shape[1] // D
    mod = pl.pallas_call(
        _ada_kernel,
        out_shape=jax.ShapeDtypeStruct((n_mod, B, D), F32),
        grid=(n_mod,),
        in_specs=[pl.BlockSpec((B, D), lambda j: (0, 0)),
                  pl.BlockSpec((D, D), lambda j: (0, j)),
                  pl.BlockSpec((1, D), lambda j: (0, j))],
        out_specs=pl.BlockSpec((1, B, D), lambda j: (j, 0, 0)),
        name="ada_mod",
    )(c, w_ada, b_ada.reshape(1, n_mod * D))
    return mod.reshape(n_mod, B, 1, D)


def _ffn_body(x, mod_ref, rows, wg_ref, wu_ref, wd_ref):
    shift, scale, gate = (mod_ref[r, 0] for r in rows)
    n = (_rms(x) * (1.0 + scale) + shift).astype(BF16)
    g = jnp.dot(n, wg_ref[...], preferred_element_type=F32)
    u = jnp.dot(n, wu_ref[...], preferred_element_type=F32)
    a = (g * _sigmoid(g) * u).astype(BF16)
    y = jnp.dot(a, wd_ref[...], preferred_element_type=F32)
    return x + (0.5 * gate) * y


def _ffn1_kernel(h_ref, mod_ref, wg_ref, wu_ref, wd_ref, o_ref):
    o_ref[0] = _ffn_body(h_ref[0], mod_ref, (0, 1, 2), wg_ref, wu_ref, wd_ref)


def _ffn2_kernel(h_ref, a_ref, att_ref, mod_ref, wo_ref, wg_ref, wu_ref, wd_ref, gf_ref, o_ref):
    c_conv = a_ref.shape[-1]
    mix = jnp.dot(a_ref[0], wo_ref[:c_conv, :], preferred_element_type=F32)
    mix += jnp.dot(att_ref[0], wo_ref[c_conv:, :], preferred_element_type=F32)
    x = h_ref[0] + mod_ref[5, 0] * mix
    out = _ffn_body(x, mod_ref, (6, 7, 8), wg_ref, wu_ref, wd_ref)
    o_ref[0] = _rms(out) * gf_ref[...]


def _mod_spec(n_mod, D):
    return pl.BlockSpec((n_mod, 1, 1, D), lambda b, i: (0, b, 0, 0))


def _ffn1(h, mod, wg, wu, wd, tm):
    B, S, D = h.shape
    F = wg.shape[1]
    tok = pl.BlockSpec((1, tm, D), lambda b, i: (b, i, 0))
    return pl.pallas_call(
        _ffn1_kernel,
        out_shape=jax.ShapeDtypeStruct((B, S, D), F32),
        grid=(B, S // tm),
        in_specs=[tok, _mod_spec(mod.shape[0], D),
                  _const_spec((D, F)), _const_spec((D, F)), _const_spec((F, D))],
        out_specs=tok,
        compiler_params=pltpu.CompilerParams(
            dimension_semantics=("parallel", "parallel"), vmem_limit_bytes=VMEM_LIMIT),
        name="ffn1",
    )(h, mod, wg, wu, wd)


def _ffn2(h, a_conv, att, mod, wo, wg, wu, wd, g_final, tm):
    B, S, D = h.shape
    F = wg.shape[1]
    C = a_conv.shape[-1]
    tok = pl.BlockSpec((1, tm, D), lambda b, i: (b, i, 0))
    half = pl.BlockSpec((1, tm, C), lambda b, i: (b, i, 0))
    return pl.pallas_call(
        _ffn2_kernel,
        out_shape=jax.ShapeDtypeStruct((B, S, D), F32),
        grid=(B, S // tm),
        in_specs=[tok, half, half, _mod_spec(mod.shape[0], D), _const_spec((D, D)),
                  _const_spec((D, F)), _const_spec((D, F)), _const_spec((F, D)),
                  _const_spec((1, D))],
        out_specs=tok,
        compiler_params=pltpu.CompilerParams(
            dimension_semantics=("parallel", "parallel"), vmem_limit_bytes=VMEM_LIMIT),
        name="mix_ffn2_norm",
    )(h, a_conv, att, mod, wo, wg, wu, wd, g_final)


def _mixer_in_kernel(h_ref, mod_ref, cos_ref, sin_ref, w_in_ref, gql_ref, gkvl_ref,
                     wq_ref, wqp_ref, wkn_ref, wv_ref, gains_ref,
                     a_ref, q_ref, k_ref, v_ref, *, c_conv, q_lora, kv_lora, n_heads):
    x = h_ref[0]
    n = (_rms(x) * (1.0 + mod_ref[4, 0]) + mod_ref[3, 0]).astype(BF16)
    z = jnp.dot(n, w_in_ref[...], preferred_element_type=F32)
    o1 = 2 * c_conv
    o2 = o1 + q_lora
    o3 = o2 + kv_lora
    a_ref[0] = z[:, :c_conv] * _sigmoid(z[:, c_conv:o1])
    cqn = (_rms(z[:, o1:o2]) * gql_ref[...]).astype(BF16)
    ckvn = (_rms(z[:, o2:o3]) * gkvl_ref[...]).astype(BF16)
    kr = z[:, o3:o3 + LANES]
    krp = z[:, o3 + LANES:o3 + 2 * LANES]
    q = jnp.dot(cqn, wq_ref[...], preferred_element_type=F32)
    qp = jnp.dot(cqn, wqp_ref[...], preferred_element_type=F32)
    kn = jnp.dot(ckvn, wkn_ref[...], preferred_element_type=F32)
    v_ref[0] = jnp.dot(ckvn, wv_ref[...], preferred_element_type=F32).astype(BF16)

    cos_f = cos_ref[0]
    sin_f = sin_ref[0]
    gq, gqp, gkn, gkr, gkrp = (gains_ref[i:i + 1, :] for i in range(5))
    kr_rot = kr * gkr * cos_f + krp * gkrp * sin_f
    kr_ssq = jnp.sum(kr * kr, axis=-1, keepdims=True)
    inv_d = 1.0 / QK_HEAD
    for hd in range(n_heads):
        sl = slice(hd * LANES, (hd + 1) * LANES)
        qh = q[:, sl]
        rq = lax.rsqrt(jnp.sum(qh * qh, axis=-1, keepdims=True) * inv_d + EPS)
        q_ref[0, :, sl] = ((qh * gq * cos_f + qp[:, sl] * gqp * sin_f)
                           * (rq * QK_HEAD ** -0.5)).astype(BF16)
        kh = kn[:, sl]
        rk = lax.rsqrt((jnp.sum(kh * kh, axis=-1, keepdims=True) + kr_ssq) * inv_d + EPS)
        k_ref[0, :, sl] = ((kh * gkn + kr_rot) * rk).astype(BF16)


def _mixer_in(h, mod, cos_f, sin_f, w_in_r, gql, gkvl, wq, wqp, wkn, wv, gains, tm,
              c_conv, q_lora, kv_lora, n_heads):
    B, S, D = h.shape
    tok = lambda w: pl.BlockSpec((1, tm, w), lambda b, i: (b, i, 0))
    hw = n_heads * LANES
    kern = functools.partial(_mixer_in_kernel, c_conv=c_conv, q_lora=q_lora,
                             kv_lora=kv_lora, n_heads=n_heads)
    return pl.pallas_call(
        kern,
        out_shape=(jax.ShapeDtypeStruct((B, S, c_conv), F32),
                   jax.ShapeDtypeStruct((B, S, hw), BF16),
                   jax.ShapeDtypeStruct((B, S, hw), BF16),
                   jax.ShapeDtypeStruct((B, S, n_heads * V_HEAD), BF16)),
        grid=(B, S // tm),
        in_specs=[tok(D), _mod_spec(mod.shape[0], D), tok(LANES), tok(LANES),
                  _const_spec(w_in_r.shape), _const_spec(gql.shape), _const_spec(gkvl.shape),
                  _const_spec(wq.shape), _const_spec(wqp.shape), _const_spec(wkn.shape),
                  _const_spec(wv.shape), _const_spec(gains.shape)],
        out_specs=(tok(c_conv), tok(hw), tok(hw), tok(n_heads * V_HEAD)),
        compiler_params=pltpu.CompilerParams(
            dimension_semantics=("parallel", "parallel"), vmem_limit_bytes=VMEM_LIMIT),
        name="mixer_in",
    )(h, mod, cos_f, sin_f, w_in_r, gql, gkvl, wq, wqp, wkn, wv, gains)


def _conv_kernel(a_ref, w_ref, b_ref, g_ref, beta_ref, o_ref, pad_ref, *, conv_k, rows):
    S, C = a_ref.shape[1], a_ref.shape[2]
    halo = jnp.zeros((CONV_HALO, C), F32)
    pad_ref[:CONV_HALO, :] = halo
    pad_ref[CONV_HALO + S:, :] = halo
    pad_ref[CONV_HALO:CONV_HALO + S, :] = a_ref[0]
    first = CONV_HALO - conv_k // 2

    span = rows + 2 * CONV_HALO - 8

    def chunk(ci, carry):
        base = pl.multiple_of(ci * rows, rows)
        cols = []
        for lt in range(C // LANES):
            ls = slice(lt * LANES, (lt + 1) * LANES)
            win = pad_ref[pl.ds(base, rows + 2 * CONV_HALO), ls]
            shifted = {r: win[r:r + span, :] for r in range(8)}
            acc = jnp.zeros((rows, LANES), F32) + b_ref[:, ls]
            for t in range(conv_k):
                off = first + t
                q8 = (off // 8) * 8
                acc += shifted[off % 8][q8:q8 + rows, :] * w_ref[t:t + 1, ls]
            cols.append(acc)
        acc = jnp.concatenate(cols, axis=-1)
        mu = jnp.mean(acc, axis=-1, keepdims=True)
        d = acc - mu
        var = jnp.mean(d * d, axis=-1, keepdims=True)
        y = d * lax.rsqrt(var + LN_EPS) * g_ref[...] + beta_ref[...]
        o_ref[0, pl.ds(base, rows), :] = (y * _sigmoid(y)).astype(o_ref.dtype)
        return carry

    lax.fori_loop(0, S // rows, chunk, 0)


def _conv_group(a, conv_w, conv_b, g_ln, b_ln, rows=64):
    B, S, C = a.shape
    conv_k = conv_w.shape[0]
    blk = pl.BlockSpec((1, S, C), lambda b: (b, 0, 0))
    return pl.pallas_call(
        functools.partial(_conv_kernel, conv_k=conv_k, rows=rows),
        out_shape=jax.ShapeDtypeStruct((B, S, C), BF16),
        grid=(B,),
        in_specs=[blk, _const_spec((conv_k, C)), _const_spec((1, C)),
                  _const_spec((1, C)), _const_spec((1, C))],
        out_specs=blk,
        scratch_shapes=[pltpu.VMEM((S + 2 * CONV_HALO, C), F32)],
        compiler_params=pltpu.CompilerParams(
            dimension_semantics=("parallel",), vmem_limit_bytes=VMEM_LIMIT),
        name="conv_group",
    )(a, conv_w, conv_b, g_ln, b_ln)


def _attn_kernel(q_ref, k_ref, v_ref, o_ref):
    v = v_ref[0]
    outs = []
    for j in range(2):
        sl = slice(j * LANES, (j + 1) * LANES)
        s = lax.dot_general(q_ref[0, :, sl], k_ref[0, :, sl], (((1,), (1,)), ((), ())),
                            preferred_element_type=F32)
        p = jnp.exp(s - jnp.max(s, axis=-1, keepdims=True))
        l = jnp.sum(p, axis=-1, keepdims=True)
        o = jnp.dot(p.astype(BF16), v, preferred_element_type=F32)
        outs.append(o / l)
    lane = lax.broadcasted_iota(jnp.int32, outs[0].shape, 1)
    o_ref[0] = jnp.where(lane < V_HEAD, outs[0], outs[1]).astype(o_ref.dtype)


def _attention(q, k, v, tq):
    B, S, hw = q.shape
    pairs = hw // (2 * LANES)
    return pl.pallas_call(
        _attn_kernel,
        out_shape=jax.ShapeDtypeStruct(v.shape, BF16),
        grid=(B, pairs, S // tq),
        in_specs=[pl.BlockSpec((1, tq, 2 * LANES), lambda b, p, i: (b, i, p)),
                  pl.BlockSpec((1, S, 2 * LANES), lambda b, p, i: (b, 0, p)),
                  pl.BlockSpec((1, S, 2 * V_HEAD), lambda b, p, i: (b, 0, p))],
        out_specs=pl.BlockSpec((1, tq, 2 * V_HEAD), lambda b, p, i: (b, i, p)),
        compiler_params=pltpu.CompilerParams(
            dimension_semantics=("parallel", "parallel", "parallel"),
            vmem_limit_bytes=VMEM_LIMIT),
        name="attention",
    )(q, k, v)


def _head_tiles(w, width):
    lead = w.shape[:-1]
    heads = w.shape[-1] // width
    w = w.reshape(*lead, heads, width)
    w = jnp.pad(w, [(0, 0)] * len(lead) + [(0, 0), (0, LANES - width)])
    return w.reshape(*lead, heads * LANES)


def _rot_partner(w):
    return jnp.concatenate([-w[..., ROPE_HALF:], w[..., :ROPE_HALF]], axis=-1)


def _rope_tile(w):
    lead = [(0, 0)] * (w.ndim - 1)
    return jnp.pad(w, lead + [(QK_NOPE, LANES - QK_HEAD)])


def kernel(x, c, positions, w_ada, b_ada, ffn1_wg, ffn1_wu, ffn1_wd, w_in, g_q_lat, g_kv_lat,
           w_q_up, w_kv_up, g_q_head, g_k_head, conv_w, conv_b, g_conv_ln, b_conv_ln, w_out,
           ffn2_wg, ffn2_wu, ffn2_wd, g_final):
    B, S, D = x.shape
    depth = w_ada.shape[0]
    c_conv = conv_b.shape[-1]
    q_lora = g_q_lat.shape[-1]
    kv_lora = g_kv_lat.shape[-1]
    n_heads = w_q_up.shape[-1] // QK_HEAD
    tm = 256

    cos_f, sin_f = _rope_tables(positions)
    h = x
    for i in range(depth):
        mod = _ada_mod(c, w_ada[i], b_ada[i])
        h = _ffn1(h, mod, ffn1_wg[i].astype(BF16), ffn1_wu[i].astype(BF16),
                  ffn1_wd[i].astype(BF16), tm)

        o3 = 2 * c_conv + q_lora + kv_lora
        w_kr = w_in[i][:, o3:]
        w_in_r = jnp.concatenate(
            [w_in[i][:, :o3], _rope_tile(w_kr), _rope_tile(_rot_partner(w_kr))], axis=-1).astype(BF16)
        wq3 = w_q_up[i].reshape(q_lora, n_heads, QK_HEAD)
        wq = _head_tiles(w_q_up[i], QK_HEAD).astype(BF16)
        wqp = _rope_tile(_rot_partner(wq3[..., QK_NOPE:])).reshape(q_lora, n_heads * LANES).astype(BF16)
        wkv3 = w_kv_up[i].reshape(kv_lora, n_heads, QK_NOPE + V_HEAD)
        wkn = _head_tiles(wkv3[..., :QK_NOPE].reshape(kv_lora, -1), QK_NOPE).astype(BF16)
        wv = wkv3[..., QK_NOPE:].reshape(kv_lora, n_heads * V_HEAD).astype(BF16)
        gq, gk = g_q_head[i], g_k_head[i]
        gains = jnp.stack([
            jnp.pad(gq, (0, LANES - QK_HEAD)),
            _rope_tile(jnp.concatenate([gq[QK_NOPE + ROPE_HALF:], gq[QK_NOPE:QK_NOPE + ROPE_HALF]])),
            jnp.pad(gk[:QK_NOPE], (0, LANES - QK_NOPE)),
            _rope_tile(gk[QK_NOPE:]),
            _rope_tile(jnp.concatenate([gk[QK_NOPE + ROPE_HALF:], gk[QK_NOPE:QK_NOPE + ROPE_HALF]])),
        ])
        gains = jnp.pad(gains, ((0, 3), (0, 0)))

        a_glu, q, k, v = _mixer_in(h, mod, cos_f, sin_f, w_in_r, g_q_lat[i].reshape(1, -1),
                                   g_kv_lat[i].reshape(1, -1), wq, wqp, wkn, wv, gains, tm,
                                   c_conv, q_lora, kv_lora, n_heads)
        a_conv = _conv_group(a_glu, conv_w[i].reshape(-1, c_conv), conv_b[i].reshape(1, -1),
                             g_conv_ln[i].reshape(1, -1), b_conv_ln[i].reshape(1, -1))
        att = _attention(q, k, v, tq=256)
        h = _ffn2(h, a_conv, att, mod, w_out[i].astype(BF16), ffn2_wg[i].astype(BF16),
                  ffn2_wu[i].astype(BF16), ffn2_wd[i].astype(BF16), g_final[i].reshape(1, -1), tm)
    return h
```

```python
import functools

import jax
import jax.numpy as jnp
from jax import lax
from jax.experimental import pallas as pl
from jax.experimental.pallas import tpu as pltpu

F32 = jnp.float32
BF16 = jnp.bfloat16

LANES = 128
EPS = 1e-6
LN_EPS = 1e-5
ROPE_THETA = 10000.0
V_HEAD = 64
QK_NOPE = 64
QK_ROPE = 32
QK_HEAD = QK_NOPE + QK_ROPE
ROPE_HALF = QK_ROPE // 2
Q_SCALE = QK_HEAD ** -0.5 * 1.4426950408889634
MXU_WIDTH = 256
PV_GROUP = MXU_WIDTH // V_HEAD
PV_WIDTH = PV_GROUP * V_HEAD
CONV_HALO = 16

VMEM_LIMIT = 56 * 1024 * 1024


def _const_spec(shape):
    zeros = (0,) * len(shape)
    return pl.BlockSpec(shape, lambda *_: zeros, pipeline_mode=pl.Buffered(1))


def _sigmoid(x):
    return 1.0 / (1.0 + jnp.exp(-x))


def _rms(x):
    return x * lax.rsqrt(jnp.mean(x * x, axis=-1, keepdims=True) + EPS)


def _rope_table_kernel(pos_ref, inv_ref, cos_ref, sin_ref):
    ang = inv_ref[...] * pos_ref[...].astype(F32)
    cos_ref[...] = jnp.cos(ang)
    sin_ref[...] = jnp.sin(ang)


def _rope_tables(positions):
    B, S = positions.shape
    n = B * S
    blk = 4096
    inv_freq = 1.0 / (ROPE_THETA ** (jnp.arange(0, QK_ROPE, 2, dtype=F32) / QK_ROPE))
    cos_t, sin_t = pl.pallas_call(
        _rope_table_kernel,
        out_shape=(jax.ShapeDtypeStruct((ROPE_HALF, n), F32),) * 2,
        grid=(n // blk,),
        in_specs=[pl.BlockSpec((1, blk), lambda i: (0, i)),
                  pl.BlockSpec((ROPE_HALF, 1), lambda i: (0, 0))],
        out_specs=(pl.BlockSpec((ROPE_HALF, blk), lambda i: (0, i)),) * 2,
        name="rope_table",
    )(positions.reshape(1, n), inv_freq.reshape(ROPE_HALF, 1))
    cos = cos_t.T.reshape(B, S, ROPE_HALF)
    sin = sin_t.T.reshape(B, S, ROPE_HALF)
    ones = jnp.ones((B, S, QK_NOPE), F32)
    zeros_n = jnp.zeros((B, S, QK_NOPE), F32)
    zeros_p = jnp.zeros((B, S, LANES - QK_HEAD), F32)
    cos_f = jnp.concatenate([ones, cos, cos, zeros_p], axis=-1)
    sin_f = jnp.concatenate([zeros_n, sin, sin, zeros_p], axis=-1)
    return cos_f, sin_f


def _ada_kernel(c_ref, w_ref, b_ref, o_ref):
    c = c_ref[...]
    sc = (c * _sigmoid(c)).astype(BF16)
    o_ref[0] = jnp.dot(sc, w_ref[...].astype(BF16), preferred_element_type=F32) + b_ref[...]


def _ada_mod(c, w_ada, b_ada):
    B, D = c.shape
    n_mod = w_ada.shape[1] // D
    mod = pl.pallas_call(
        _ada_kernel,
        out_shape=jax.ShapeDtypeStruct((n_mod, B, D), F32),
        grid=(n_mod,),
        in_specs=[pl.BlockSpec((B, D), lambda j: (0, 0)),
                  pl.BlockSpec((D, D), lambda j: (0, j)),
                  pl.BlockSpec((1, D), lambda j: (0, j))],
        out_specs=pl.BlockSpec((1, B, D), lambda j: (j, 0, 0)),
        name="ada_mod",
    )(c, w_ada, b_ada.reshape(1, n_mod * D))
    return mod.reshape(n_mod, B, 1, D)


def _ffn_body(x, mod_ref, rows, wg_ref, wu_ref, wd_ref):
    shift, scale, gate = (mod_ref[r, 0] for r in rows)
    n = (_rms(x) * (1.0 + scale) + shift).astype(BF16)
    g = jnp.dot(n, wg_ref[...], preferred_element_type=F32)
    u = jnp.dot(n, wu_ref[...], preferred_element_type=F32)
    a = (g * _sigmoid(g) * u).astype(BF16)
    y = jnp.dot(a, wd_ref[...], preferred_element_type=F32)
    return x + (0.5 * gate) * y


def _ffn1_kernel(h_ref, mod_ref, wg_ref, wu_ref, wd_ref, o_ref):
    o_ref[0] = _ffn_body(h_ref[0], mod_ref, (0, 1, 2), wg_ref, wu_ref, wd_ref)


def _ffn2_kernel(h_ref, a_ref, att_ref, mod_ref, wo_ref, wg_ref, wu_ref, wd_ref, gf_ref, o_ref):
    c_conv = a_ref.shape[-1]
    mix = jnp.dot(a_ref[0], wo_ref[:c_conv, :], preferred_element_type=F32)
    mix += jnp.dot(att_ref[0], wo_ref[c_conv:, :], preferred_element_type=F32)
    x = h_ref[0] + mod_ref[5, 0] * mix
    out = _ffn_body(x, mod_ref, (6, 7, 8), wg_ref, wu_ref, wd_ref)
    o_ref[0] = _rms(out) * gf_ref[...]


def _mod_spec(n_mod, D):
    return pl.BlockSpec((n_mod, 1, 1, D), lambda b, i: (0, b, 0, 0))


def _ffn1(h, mod, wg, wu, wd, tm):
    B, S, D = h.shape
    F = wg.shape[1]
    tok = pl.BlockSpec((1, tm, D), lambda b, i: (b, i, 0))
    return pl.pallas_call(
        _ffn1_kernel,
        out_shape=jax.ShapeDtypeStruct((B, S, D), F32),
        grid=(B, S // tm),
        in_specs=[tok, _mod_spec(mod.shape[0], D),
                  _const_spec((D, F)), _const_spec((D, F)), _const_spec((F, D))],
        out_specs=tok,
        compiler_params=pltpu.CompilerParams(
            dimension_semantics=("parallel", "parallel"), vmem_limit_bytes=VMEM_LIMIT),
        name="ffn1",
    )(h, mod, wg, wu, wd)


def _ffn2(h, a_conv, att, mod, wo, wg, wu, wd, g_final, tm):
    B, S, D = h.shape
    F = wg.shape[1]
    C = a_conv.shape[-1]
    tok = pl.BlockSpec((1, tm, D), lambda b, i: (b, i, 0))
    half = pl.BlockSpec((1, tm, C), lambda b, i: (b, i, 0))
    return pl.pallas_call(
        _ffn2_kernel,
        out_shape=jax.ShapeDtypeStruct((B, S, D), F32),
        grid=(B, S // tm),
        in_specs=[tok, half, half, _mod_spec(mod.shape[0], D), _const_spec((D, D)),
                  _const_spec((D, F)), _const_spec((D, F)), _const_spec((F, D)),
                  _const_spec((1, D))],
        out_specs=tok,
        compiler_params=pltpu.CompilerParams(
            dimension_semantics=("parallel", "parallel"), vmem_limit_bytes=VMEM_LIMIT),
        name="mix_ffn2_norm",
    )(h, a_conv, att, mod, wo, wg, wu, wd, g_final)


def _mixer_in_kernel(h_ref, mod_ref, cos_ref, sin_ref, w_in_ref, gql_ref, gkvl_ref,
                     wq_ref, wqp_ref, wkn_ref, wv_ref, gains_ref,
                     a_ref, q_ref, k_ref, v_ref, *, c_conv, q_lora, kv_lora, n_heads):
    x = h_ref[0]
    n = (_rms(x) * (1.0 + mod_ref[4, 0]) + mod_ref[3, 0]).astype(BF16)
    z = jnp.dot(n, w_in_ref[...], preferred_element_type=F32)
    o1 = 2 * c_conv
    o2 = o1 + q_lora
    o3 = o2 + kv_lora
    a_ref[0] = z[:, :c_conv] * _sigmoid(z[:, c_conv:o1])
    cqn = (_rms(z[:, o1:o2]) * gql_ref[...]).astype(BF16)
    ckvn = (_rms(z[:, o2:o3]) * gkvl_ref[...]).astype(BF16)
    kr = z[:, o3:o3 + LANES]
    krp = z[:, o3 + LANES:o3 + 2 * LANES]
    q = jnp.dot(cqn, wq_ref[...], preferred_element_type=F32)
    qp = jnp.dot(cqn, wqp_ref[...], preferred_element_type=F32)
    kn = jnp.dot(ckvn, wkn_ref[...], preferred_element_type=F32)
    v_ref[0] = jnp.dot(ckvn, wv_ref[...], preferred_element_type=F32).astype(BF16)

    cos_f = cos_ref[0]
    sin_f = sin_ref[0]
    gq, gqp, gkn, gkr, gkrp = (gains_ref[i:i + 1, :] for i in range(5))
    kr_rot = kr * gkr * cos_f + krp * gkrp * sin_f
    kr_ssq = jnp.sum(kr * kr, axis=-1, keepdims=True)
    inv_d = 1.0 / QK_HEAD
    for hd in range(n_heads):
        sl = slice(hd * LANES, (hd + 1) * LANES)
        qh = q[:, sl]
        rq = lax.rsqrt(jnp.sum(qh * qh, axis=-1, keepdims=True) * inv_d + EPS)
        q_ref[0, :, sl] = ((qh * gq * cos_f + qp[:, sl] * gqp * sin_f)
                           * (rq * Q_SCALE)).astype(BF16)
        kh = kn[:, sl]
        rk = lax.rsqrt((jnp.sum(kh * kh, axis=-1, keepdims=True) + kr_ssq) * inv_d + EPS)
        k_ref[0, :, sl] = ((kh * gkn + kr_rot) * rk).astype(BF16)


def _mixer_in(h, mod, cos_f, sin_f, w_in_r, gql, gkvl, wq, wqp, wkn, wv, gains, tm,
              c_conv, q_lora, kv_lora, n_heads):
    B, S, D = h.shape
    tok = lambda w: pl.BlockSpec((1, tm, w), lambda b, i: (b, i, 0))
    hw = n_heads * LANES
    kern = functools.partial(_mixer_in_kernel, c_conv=c_conv, q_lora=q_lora,
                             kv_lora=kv_lora, n_heads=n_heads)
    return pl.pallas_call(
        kern,
        out_shape=(jax.ShapeDtypeStruct((B, S, c_conv), F32),
                   jax.ShapeDtypeStruct((B, S, hw), BF16),
                   jax.ShapeDtypeStruct((B, S, hw), BF16),
                   jax.ShapeDtypeStruct((B, S, n_heads * V_HEAD), BF16)),
        grid=(B, S // tm),
        in_specs=[tok(D), _mod_spec(mod.shape[0], D), tok(LANES), tok(LANES),
                  _const_spec(w_in_r.shape), _const_spec(gql.shape), _const_spec(gkvl.shape),
                  _const_spec(wq.shape), _const_spec(wqp.shape), _const_spec(wkn.shape),
                  _const_spec(wv.shape), _const_spec(gains.shape)],
        out_specs=(tok(c_conv), tok(hw), tok(hw), tok(n_heads * V_HEAD)),
        compiler_params=pltpu.CompilerParams(
            dimension_semantics=("parallel", "parallel"), vmem_limit_bytes=VMEM_LIMIT),
        name="mixer_in",
    )(h, mod, cos_f, sin_f, w_in_r, gql, gkvl, wq, wqp, wkn, wv, gains)


def _conv_kernel(a_ref, w_ref, b_ref, g_ref, beta_ref, o_ref, pad_ref, *, conv_k, rows):
    S, C = a_ref.shape[1], a_ref.shape[2]
    halo = jnp.zeros((CONV_HALO, C), F32)
    pad_ref[:CONV_HALO, :] = halo
    pad_ref[CONV_HALO + S:, :] = halo
    pad_ref[CONV_HALO:CONV_HALO + S, :] = a_ref[0]
    first = CONV_HALO - conv_k // 2

    span = rows + 2 * CONV_HALO - 8

    def chunk(ci, carry):
        base = pl.multiple_of(ci * rows, rows)
        cols = []
        for lt in range(C // LANES):
            ls = slice(lt * LANES, (lt + 1) * LANES)
            win = pad_ref[pl.ds(base, rows + 2 * CONV_HALO), ls]
            shifted = {r: win[r:r + span, :] for r in range(8)}
            acc = jnp.zeros((rows, LANES), F32) + b_ref[:, ls]
            for t in range(conv_k):
                off = first + t
                q8 = (off // 8) * 8
                acc += shifted[off % 8][q8:q8 + rows, :] * w_ref[t:t + 1, ls]
            cols.append(acc)
        acc = jnp.concatenate(cols, axis=-1)
        mu = jnp.mean(acc, axis=-1, keepdims=True)
        d = acc - mu
        var = jnp.mean(d * d, axis=-1, keepdims=True)
        y = d * lax.rsqrt(var + LN_EPS) * g_ref[...] + beta_ref[...]
        o_ref[0, pl.ds(base, rows), :] = (y * _sigmoid(y)).astype(o_ref.dtype)
        return carry

    lax.fori_loop(0, S // rows, chunk, 0)


def _conv_group(a, conv_w, conv_b, g_ln, b_ln, rows=64):
    B, S, C = a.shape
    conv_k = conv_w.shape[0]
    blk = pl.BlockSpec((1, S, C), lambda b: (b, 0, 0))
    return pl.pallas_call(
        functools.partial(_conv_kernel, conv_k=conv_k, rows=rows),
        out_shape=jax.ShapeDtypeStruct((B, S, C), BF16),
        grid=(B,),
        in_specs=[blk, _const_spec((conv_k, C)), _const_spec((1, C)),
                  _const_spec((1, C)), _const_spec((1, C))],
        out_specs=blk,
        scratch_shapes=[pltpu.VMEM((S + 2 * CONV_HALO, C), F32)],
        compiler_params=pltpu.CompilerParams(
            dimension_semantics=("parallel",), vmem_limit_bytes=VMEM_LIMIT),
        name="conv_group",
    )(a, conv_w, conv_b, g_ln, b_ln)


def _attn_kernel(q_ref, k_ref, v_ref, o_ref, *, n_heads):
    def scores(hd):
        sl = slice(hd * LANES, (hd + 1) * LANES)
        return lax.dot_general(q_ref[0, :, sl], k_ref[0, :, sl], (((1,), (1,)), ((), ())),
                               preferred_element_type=F32)

    def attend(hd, s):
        tiles = [s[:, c * LANES:(c + 1) * LANES] for c in range(s.shape[1] // LANES)]
        m_part = tiles[0]
        for t in tiles[1:]:
            m_part = jnp.maximum(m_part, t)
        m = jnp.max(m_part, axis=-1, keepdims=True)
        l_part = jnp.zeros_like(m_part)
        ps = []
        for t in tiles:
            pc = jnp.exp2(t - m)
            l_part += pc
            ps.append(pc.astype(BF16))
        l = jnp.sum(l_part, axis=-1, keepdims=True)
        grp = slice((hd // PV_GROUP) * PV_WIDTH, (hd // PV_GROUP + 1) * PV_WIDTH)
        o = jnp.dot(jnp.concatenate(ps, axis=-1), v_ref[0, :, grp], preferred_element_type=F32)
        return o / l

    s_next = scores(0)
    outs = []
    for hd in range(n_heads):
        s = s_next
        if hd + 1 < n_heads:
            s_next = scores(hd + 1)
        outs.append(attend(hd, s))
    slot = lax.broadcasted_iota(jnp.int32, outs[0].shape, 1) // V_HEAD
    for g in range(n_heads // PV_GROUP):
        o = outs[g * PV_GROUP]
        for j in range(1, PV_GROUP):
            o = jnp.where(slot == j, outs[g * PV_GROUP + j], o)
        o_ref[0, :, g * PV_WIDTH:(g + 1) * PV_WIDTH] = o.astype(o_ref.dtype)


def _attention(q, k, v, tq):
    B, S, hw = q.shape
    vw = v.shape[-1]
    n_heads = hw // LANES
    return pl.pallas_call(
        functools.partial(_attn_kernel, n_heads=n_heads),
        out_shape=jax.ShapeDtypeStruct(v.shape, BF16),
        grid=(B, S // tq),
        in_specs=[pl.BlockSpec((1, tq, hw), lambda b, i: (b, i, 0)),
                  pl.BlockSpec((1, S, hw), lambda b, i: (b, 0, 0)),
                  pl.BlockSpec((1, S, vw), lambda b, i: (b, 0, 0))],
        out_specs=pl.BlockSpec((1, tq, vw), lambda b, i: (b, i, 0)),
        compiler_params=pltpu.CompilerParams(
            dimension_semantics=("parallel", "parallel"), vmem_limit_bytes=VMEM_LIMIT),
        name="attention",
    )(q, k, v)


def _head_tiles(w, width):
    lead = w.shape[:-1]
    heads = w.shape[-1] // width
    w = w.reshape(*lead, heads, width)
    w = jnp.pad(w, [(0, 0)] * len(lead) + [(0, 0), (0, LANES - width)])
    return w.reshape(*lead, heads * LANES)


def _rot_partner(w):
    return jnp.concatenate([-w[..., ROPE_HALF:], w[..., :ROPE_HALF]], axis=-1)


def _rope_tile(w):
    lead = [(0, 0)] * (w.ndim - 1)
    return jnp.pad(w, lead + [(QK_NOPE, LANES - QK_HEAD)])


def kernel(x, c, positions, w_ada, b_ada, ffn1_wg, ffn1_wu, ffn1_wd, w_in, g_q_lat, g_kv_lat,
           w_q_up, w_kv_up, g_q_head, g_k_head, conv_w, conv_b, g_conv_ln, b_conv_ln, w_out,
           ffn2_wg, ffn2_wu, ffn2_wd, g_final):
    B, S, D = x.shape
    depth = w_ada.shape[0]
    c_conv = conv_b.shape[-1]
    q_lora = g_q_lat.shape[-1]
    kv_lora = g_kv_lat.shape[-1]
    n_heads = w_q_up.shape[-1] // QK_HEAD
    tm = 256

    cos_f, sin_f = _rope_tables(positions)
    h = x
    for i in range(depth):
        mod = _ada_mod(c, w_ada[i], b_ada[i])
        h = _ffn1(h, mod, ffn1_wg[i].astype(BF16), ffn1_wu[i].astype(BF16),
                  ffn1_wd[i].astype(BF16), tm)

        o3 = 2 * c_conv + q_lora + kv_lora
        w_kr = w_in[i][:, o3:]
        w_in_r = jnp.concatenate(
            [w_in[i][:, :o3], _rope_tile(w_kr), _rope_tile(_rot_partner(w_kr))], axis=-1).astype(BF16)
        wq3 = w_q_up[i].reshape(q_lora, n_heads, QK_HEAD)
        wq = _head_tiles(w_q_up[i], QK_HEAD).astype(BF16)
        wqp = _rope_tile(_rot_partner(wq3[..., QK_NOPE:])).reshape(q_lora, n_heads * LANES).astype(BF16)
        wkv3 = w_kv_up[i].reshape(kv_lora, n_heads, QK_NOPE + V_HEAD)
        wkn = _head_tiles(wkv3[..., :QK_NOPE].reshape(kv_lora, -1), QK_NOPE).astype(BF16)
        wv = wkv3[..., QK_NOPE:].reshape(kv_lora, n_heads * V_HEAD).astype(BF16)
        gq, gk = g_q_head[i], g_k_head[i]
        gains = jnp.stack([
            jnp.pad(gq, (0, LANES - QK_HEAD)),
            _rope_tile(jnp.concatenate([gq[QK_NOPE + ROPE_HALF:], gq[QK_NOPE:QK_NOPE + ROPE_HALF]])),
            jnp.pad(gk[:QK_NOPE], (0, LANES - QK_NOPE)),
            _rope_tile(gk[QK_NOPE:]),
            _rope_tile(jnp.concatenate([gk[QK_NOPE + ROPE_HALF:], gk[QK_NOPE:QK_NOPE + ROPE_HALF]])),
        ])
        gains = jnp.pad(gains, ((0, 3), (0, 0)))

        a_glu, q, k, v = _mixer_in(h, mod, cos_f, sin_f, w_in_r, g_q_lat[i].reshape(1, -1),
                                   g_kv_lat[i].reshape(1, -1), wq, wqp, wkn, wv, gains, tm,
                                   c_conv, q_lora, kv_lora, n_heads)
        a_conv = _conv_group(a_glu, conv_w[i].reshape(-1, c_conv), conv_b[i].reshape(1, -1),
                             g_conv_ln[i].reshape(1, -1), b_conv_ln[i].reshape(1, -1))
        att = _attention(q, k, v, tq=512)
        h = _ffn2(h, a_conv, att, mod, w_out[i].astype(BF16), ffn2_wg[i].astype(BF16),
                  ffn2_wu[i].astype(BF16), ffn2_wd[i].astype(BF16), g_final[i].reshape(1, -1), tm)
    return h
```

```python
import functools

import jax
import jax.numpy as jnp
from jax import lax
from jax.experimental import pallas as pl
from jax.experimental.pallas import tpu as pltpu

F32 = jnp.float32
BF16 = jnp.bfloat16

LANES = 128
EPS = 1e-6
LN_EPS = 1e-5
ROPE_THETA = 10000.0
V_HEAD = 64
QK_NOPE = 64
QK_ROPE = 32
QK_HEAD = QK_NOPE + QK_ROPE
ROPE_HALF = QK_ROPE // 2
Q_SCALE = QK_HEAD ** -0.5 * 1.4426950408889634
MXU_WIDTH = 256
PV_GROUP = MXU_WIDTH // V_HEAD
PV_WIDTH = PV_GROUP * V_HEAD
CONV_HALO = 16

VMEM_LIMIT = 56 * 1024 * 1024


def _const_spec(shape):
    zeros = (0,) * len(shape)
    return pl.BlockSpec(shape, lambda *_: zeros, pipeline_mode=pl.Buffered(1))


def _sigmoid(x):
    return 1.0 / (1.0 + jnp.exp(-x))


def _rms(x):
    return x * lax.rsqrt(jnp.mean(x * x, axis=-1, keepdims=True) + EPS)


def _rope_table_kernel(pos_ref, inv_ref, cos_ref, sin_ref):
    ang = inv_ref[...] * pos_ref[...].astype(F32)
    cos_ref[...] = jnp.cos(ang)
    sin_ref[...] = jnp.sin(ang)


def _rope_tables(positions):
    B, S = positions.shape
    n = B * S
    blk = 4096
    inv_freq = 1.0 / (ROPE_THETA ** (jnp.arange(0, QK_ROPE, 2, dtype=F32) / QK_ROPE))
    cos_t, sin_t = pl.pallas_call(
        _rope_table_kernel,
        out_shape=(jax.ShapeDtypeStruct((ROPE_HALF, n), F32),) * 2,
        grid=(n // blk,),
        in_specs=[pl.BlockSpec((1, blk), lambda i: (0, i)),
                  pl.BlockSpec((ROPE_HALF, 1), lambda i: (0, 0))],
        out_specs=(pl.BlockSpec((ROPE_HALF, blk), lambda i: (0, i)),) * 2,
        name="rope_table",
    )(positions.reshape(1, n), inv_freq.reshape(ROPE_HALF, 1))
    cos = cos_t.T.reshape(B, S, ROPE_HALF)
    sin = sin_t.T.reshape(B, S, ROPE_HALF)
    ones = jnp.ones((B, S, QK_NOPE), F32)
    zeros_n = jnp.zeros((B, S, QK_NOPE), F32)
    zeros_p = jnp.zeros((B, S, LANES - QK_HEAD), F32)
    cos_f = jnp.concatenate([ones, cos, cos, zeros_p], axis=-1)
    sin_f = jnp.concatenate([zeros_n, sin, sin, zeros_p], axis=-1)
    return cos_f, sin_f


def _ada_kernel(c_ref, w_ref, b_ref, o_ref):
    c = c_ref[...]
    sc = (c * _sigmoid(c)).astype(BF16)
    o_ref[0] = jnp.dot(sc, w_ref[...].astype(BF16), preferred_element_type=F32) + b_ref[...]


def _ada_mod(c, w_ada, b_ada):
    B, D = c.shape
    n_mod = w_ada.shape[1] // D
    mod = pl.pallas_call(
        _ada_kernel,
        out_shape=jax.ShapeDtypeStruct((n_mod, B, D), F32),
        grid=(n_mod,),
        in_specs=[pl.BlockSpec((B, D), lambda j: (0, 0)),
                  pl.BlockSpec((D, D), lambda j: (0, j)),
                  pl.BlockSpec((1, D), lambda j: (0, j))],
        out_specs=pl.BlockSpec((1, B, D), lambda j: (j, 0, 0)),
        name="ada_mod",
    )(c, w_ada, b_ada.reshape(1, n_mod * D))
    return mod.reshape(n_mod, B, 1, D)


def _ffn_body(x, mod_ref, rows, wg_ref, wu_ref, wd_ref):
    shift, scale, gate = (mod_ref[r, 0] for r in rows)
    n = (_rms(x) * (1.0 + scale) + shift).astype(BF16)
    g = jnp.dot(n, wg_ref[...], preferred_element_type=F32)
    u = jnp.dot(n, wu_ref[...], preferred_element_type=F32)
    a = (g * _sigmoid(g) * u).astype(BF16)
    y = jnp.dot(a, wd_ref[...], preferred_element_type=F32)
    return x + (0.5 * gate) * y


def _ffn1_kernel(h_ref, mod_ref, wg_ref, wu_ref, wd_ref, o_ref):
    o_ref[0] = _ffn_body(h_ref[0], mod_ref, (0, 1, 2), wg_ref, wu_ref, wd_ref)


def _ffn2_kernel(h_ref, a_ref, att_ref, mod_ref, wo_ref, wg_ref, wu_ref, wd_ref, gf_ref, o_ref):
    c_conv = a_ref.shape[-1]
    mix = jnp.dot(a_ref[0], wo_ref[:c_conv, :], preferred_element_type=F32)
    mix += jnp.dot(att_ref[0], wo_ref[c_conv:, :], preferred_element_type=F32)
    x = h_ref[0] + mod_ref[5, 0] * mix
    out = _ffn_body(x, mod_ref, (6, 7, 8), wg_ref, wu_ref, wd_ref)
    o_ref[0] = _rms(out) * gf_ref[...]


def _mod_spec(n_mod, D):
    return pl.BlockSpec((n_mod, 1, 1, D), lambda b, i: (0, b, 0, 0))


def _ffn1(h, mod, wg, wu, wd, tm):
    B, S, D = h.shape
    F = wg.shape[1]
    tok = pl.BlockSpec((1, tm, D), lambda b, i: (b, i, 0))
    return pl.pallas_call(
        _ffn1_kernel,
        out_shape=jax.ShapeDtypeStruct((B, S, D), F32),
        grid=(B, S // tm),
        in_specs=[tok, _mod_spec(mod.shape[0], D),
                  _const_spec((D, F)), _const_spec((D, F)), _const_spec((F, D))],
        out_specs=tok,
        compiler_params=pltpu.CompilerParams(
            dimension_semantics=("parallel", "parallel"), vmem_limit_bytes=VMEM_LIMIT),
        name="ffn1",
    )(h, mod, wg, wu, wd)


def _ffn2(h, a_conv, att, mod, wo, wg, wu, wd, g_final, tm):
    B, S, D = h.shape
    F = wg.shape[1]
    C = a_conv.shape[-1]
    tok = pl.BlockSpec((1, tm, D), lambda b, i: (b, i, 0))
    half = pl.BlockSpec((1, tm, C), lambda b, i: (b, i, 0))
    return pl.pallas_call(
        _ffn2_kernel,
        out_shape=jax.ShapeDtypeStruct((B, S, D), F32),
        grid=(B, S // tm),
        in_specs=[tok, half, half, _mod_spec(mod.shape[0], D), _const_spec((D, D)),
                  _const_spec((D, F)), _const_spec((D, F)), _const_spec((F, D)),
                  _const_spec((1, D))],
        out_specs=tok,
        compiler_params=pltpu.CompilerParams(
            dimension_semantics=("parallel", "parallel"), vmem_limit_bytes=VMEM_LIMIT),
        name="mix_ffn2_norm",
    )(h, a_conv, att, mod, wo, wg, wu, wd, g_final)


def _mixer_in_kernel(h_ref, mod_ref, cos_ref, sin_ref, w_in_ref, gql_ref, gkvl_ref,
                     wq_ref, wqp_ref, wkn_ref, wv_ref, gains_ref,
                     a_ref, q_ref, k_ref, v_ref, *, c_conv, q_lora, kv_lora, n_heads):
    x = h_ref[0]
    n = (_rms(x) * (1.0 + mod_ref[4, 0]) + mod_ref[3, 0]).astype(BF16)
    z = jnp.dot(n, w_in_ref[...], preferred_element_type=F32)
    o1 = 2 * c_conv
    o2 = o1 + q_lora
    o3 = o2 + kv_lora
    a_ref[0] = z[:, :c_conv] * _sigmoid(z[:, c_conv:o1])
    cqn = (_rms(z[:, o1:o2]) * gql_ref[...]).astype(BF16)
    ckvn = (_rms(z[:, o2:o3]) * gkvl_ref[...]).astype(BF16)
    kr = z[:, o3:o3 + LANES]
    krp = z[:, o3 + LANES:o3 + 2 * LANES]
    q = jnp.dot(cqn, wq_ref[...], preferred_element_type=F32)
    qp = jnp.dot(cqn, wqp_ref[...], preferred_element_type=F32)
    kn = jnp.dot(ckvn, wkn_ref[...], preferred_element_type=F32)
    v_ref[0] = jnp.dot(ckvn, wv_ref[...], preferred_element_type=F32).astype(BF16)

    cos_f = cos_ref[0]
    sin_f = sin_ref[0]
    gq, gqp, gkn, gkr, gkrp = (gains_ref[i:i + 1, :] for i in range(5))
    kr_rot = kr * gkr * cos_f + krp * gkrp * sin_f
    kr_ssq = jnp.sum(kr * kr, axis=-1, keepdims=True)
    q_cos = gq * Q_SCALE * cos_f
    q_sin = gqp * Q_SCALE * sin_f
    inv_d = 1.0 / QK_HEAD
    for hd in range(n_heads):
        sl = slice(hd * LANES, (hd + 1) * LANES)
        qh = q[:, sl]
        rq = lax.rsqrt(jnp.sum(qh * qh, axis=-1, keepdims=True) * inv_d + EPS)
        q_ref[0, :, sl] = ((qh * q_cos + qp[:, sl] * q_sin) * rq).astype(BF16)
        kh = kn[:, sl]
        rk = lax.rsqrt((jnp.sum(kh * kh, axis=-1, keepdims=True) + kr_ssq) * inv_d + EPS)
        k_ref[0, :, sl] = ((kh * gkn + kr_rot) * rk).astype(BF16)


def _mixer_in(h, mod, cos_f, sin_f, w_in_r, gql, gkvl, wq, wqp, wkn, wv, gains, tm,
              c_conv, q_lora, kv_lora, n_heads):
    B, S, D = h.shape
    tok = lambda w: pl.BlockSpec((1, tm, w), lambda b, i: (b, i, 0))
    hw = n_heads * LANES
    kern = functools.partial(_mixer_in_kernel, c_conv=c_conv, q_lora=q_lora,
                             kv_lora=kv_lora, n_heads=n_heads)
    return pl.pallas_call(
        kern,
        out_shape=(jax.ShapeDtypeStruct((B, S, c_conv), F32),
                   jax.ShapeDtypeStruct((B, S, hw), BF16),
                   jax.ShapeDtypeStruct((B, S, hw), BF16),
                   jax.ShapeDtypeStruct((B, S, n_heads * V_HEAD), BF16)),
        grid=(B, S // tm),
        in_specs=[tok(D), _mod_spec(mod.shape[0], D), tok(LANES), tok(LANES),
                  _const_spec(w_in_r.shape), _const_spec(gql.shape), _const_spec(gkvl.shape),
                  _const_spec(wq.shape), _const_spec(wqp.shape), _const_spec(wkn.shape),
                  _const_spec(wv.shape), _const_spec(gains.shape)],
        out_specs=(tok(c_conv), tok(hw), tok(hw), tok(n_heads * V_HEAD)),
        compiler_params=pltpu.CompilerParams(
            dimension_semantics=("parallel", "parallel"), vmem_limit_bytes=VMEM_LIMIT),
        name="mixer_in",
    )(h, mod, cos_f, sin_f, w_in_r, gql, gkvl, wq, wqp, wkn, wv, gains)


def _conv_kernel(a_ref, w_ref, b_ref, g_ref, beta_ref, o_ref, pad_ref, shift_ref, *, conv_k, rows):
    S, C = a_ref.shape[1], a_ref.shape[2]
    halo = jnp.zeros((CONV_HALO, C), F32)
    pad_ref[:CONV_HALO, :] = halo
    pad_ref[CONV_HALO + S:, :] = halo
    pad_ref[CONV_HALO:CONV_HALO + S, :] = a_ref[0]
    first = CONV_HALO - conv_k // 2

    span = rows + 2 * CONV_HALO - 8

    def chunk(ci, carry):
        base = pl.multiple_of(ci * rows, rows)
        cols = []
        for lt in range(C // LANES):
            ls = slice(lt * LANES, (lt + 1) * LANES)
            win = pad_ref[pl.ds(base, rows + 2 * CONV_HALO), ls]
            for r in range(1, 8):
                shift_ref[r, :, ls] = win[r:r + span, :]
            acc = jnp.zeros((rows, LANES), F32) + b_ref[:, ls]
            for t in range(conv_k):
                r = (first + t) % 8
                q8 = first + t - r
                tap = win[q8:q8 + rows, :] if r == 0 else shift_ref[r, q8:q8 + rows, ls]
                acc += tap * w_ref[t:t + 1, ls]
            cols.append(acc)
        acc = jnp.concatenate(cols, axis=-1)
        mu = jnp.mean(acc, axis=-1, keepdims=True)
        d = acc - mu
        var = jnp.mean(d * d, axis=-1, keepdims=True)
        y = d * lax.rsqrt(var + LN_EPS) * g_ref[...] + beta_ref[...]
        o_ref[0, pl.ds(base, rows), :] = (y * _sigmoid(y)).astype(o_ref.dtype)
        return carry

    lax.fori_loop(0, S // rows, chunk, 0)


def _conv_group(a, conv_w, conv_b, g_ln, b_ln, rows=64):
    B, S, C = a.shape
    conv_k = conv_w.shape[0]
    blk = pl.BlockSpec((1, S, C), lambda b: (b, 0, 0))
    return pl.pallas_call(
        functools.partial(_conv_kernel, conv_k=conv_k, rows=rows),
        out_shape=jax.ShapeDtypeStruct((B, S, C), BF16),
        grid=(B,),
        in_specs=[blk, _const_spec((conv_k, C)), _const_spec((1, C)),
                  _const_spec((1, C)), _const_spec((1, C))],
        out_specs=blk,
        scratch_shapes=[pltpu.VMEM((S + 2 * CONV_HALO, C), F32),
                        pltpu.VMEM((8, rows + 2 * CONV_HALO - 8, C), F32)],
        compiler_params=pltpu.CompilerParams(
            dimension_semantics=("parallel",), vmem_limit_bytes=VMEM_LIMIT),
        name="conv_group",
    )(a, conv_w, conv_b, g_ln, b_ln)


def _attn_kernel(q_ref, k_ref, v_ref, o_ref, *, n_heads):
    def scores(hd):
        sl = slice(hd * LANES, (hd + 1) * LANES)
        return lax.dot_general(q_ref[0, :, sl], k_ref[0, :, sl], (((1,), (1,)), ((), ())),
                               preferred_element_type=F32)

    def attend(hd, s):
        tiles = [s[:, c * LANES:(c + 1) * LANES] for c in range(s.shape[1] // LANES)]
        m_part = tiles[0]
        for t in tiles[1:]:
            m_part = jnp.maximum(m_part, t)
        m = jnp.max(m_part, axis=-1, keepdims=True)
        l_part = jnp.zeros_like(m_part)
        ps = []
        for t in tiles:
            pc = jnp.exp2(t - m)
            l_part += pc
            ps.append(pc.astype(BF16))
        l = jnp.sum(l_part, axis=-1, keepdims=True)
        grp = slice((hd // PV_GROUP) * PV_WIDTH, (hd // PV_GROUP + 1) * PV_WIDTH)
        o = jnp.dot(jnp.concatenate(ps, axis=-1), v_ref[0, :, grp], preferred_element_type=F32)
        return o / l

    s_next = scores(0)
    outs = []
    for hd in range(n_heads):
        s = s_next
        if hd + 1 < n_heads:
            s_next = scores(hd + 1)
        outs.append(attend(hd, s))
    slot = lax.broadcasted_iota(jnp.int32, outs[0].shape, 1) // V_HEAD
    for g in range(n_heads // PV_GROUP):
        o = outs[g * PV_GROUP]
        for j in range(1, PV_GROUP):
            o = jnp.where(slot == j, outs[g * PV_GROUP + j], o)
        o_ref[0, :, g * PV_WIDTH:(g + 1) * PV_WIDTH] = o.astype(o_ref.dtype)


def _attention(q, k, v, tq):
    B, S, hw = q.shape
    vw = v.shape[-1]
    n_heads = hw // LANES
    return pl.pallas_call(
        functools.partial(_attn_kernel, n_heads=n_heads),
        out_shape=jax.ShapeDtypeStruct(v.shape, BF16),
        grid=(B, S // tq),
        in_specs=[pl.BlockSpec((1, tq, hw), lambda b, i: (b, i, 0)),
                  pl.BlockSpec((1, S, hw), lambda b, i: (b, 0, 0)),
                  pl.BlockSpec((1, S, vw), lambda b, i: (b, 0, 0))],
        out_specs=pl.BlockSpec((1, tq, vw), lambda b, i: (b, i, 0)),
        compiler_params=pltpu.CompilerParams(
            dimension_semantics=("parallel", "parallel"), vmem_limit_bytes=VMEM_LIMIT),
        name="attention",
    )(q, k, v)


def _head_tiles(w, width):
    lead = w.shape[:-1]
    heads = w.shape[-1] // width
    w = w.reshape(*lead, heads, width)
    w = jnp.pad(w, [(0, 0)] * len(lead) + [(0, 0), (0, LANES - width)])
    return w.reshape(*lead, heads * LANES)


def _rot_partner(w):
    return jnp.concatenate([-w[..., ROPE_HALF:], w[..., :ROPE_HALF]], axis=-1)


def _rope_tile(w):
    lead = [(0, 0)] * (w.ndim - 1)
    return jnp.pad(w, lead + [(QK_NOPE, LANES - QK_HEAD)])


def kernel(x, c, positions, w_ada, b_ada, ffn1_wg, ffn1_wu, ffn1_wd, w_in, g_q_lat, g_kv_lat,
           w_q_up, w_kv_up, g_q_head, g_k_head, conv_w, conv_b, g_conv_ln, b_conv_ln, w_out,
           ffn2_wg, ffn2_wu, ffn2_wd, g_final):
    B, S, D = x.shape
    depth = w_ada.shape[0]
    c_conv = conv_b.shape[-1]
    q_lora = g_q_lat.shape[-1]
    kv_lora = g_kv_lat.shape[-1]
    n_heads = w_q_up.shape[-1] // QK_HEAD
    tm = 512

    cos_f, sin_f = _rope_tables(positions)
    h = x
    for i in range(depth):
        mod = _ada_mod(c, w_ada[i], b_ada[i])
        h = _ffn1(h, mod, ffn1_wg[i].astype(BF16), ffn1_wu[i].astype(BF16),
                  ffn1_wd[i].astype(BF16), tm)

        o3 = 2 * c_conv + q_lora + kv_lora
        w_kr = w_in[i][:, o3:]
        w_in_r = jnp.concatenate(
            [w_in[i][:, :o3], _rope_tile(w_kr), _rope_tile(_rot_partner(w_kr))], axis=-1).astype(BF16)
        wq3 = w_q_up[i].reshape(q_lora, n_heads, QK_HEAD)
        wq = _head_tiles(w_q_up[i], QK_HEAD).astype(BF16)
        wqp = _rope_tile(_rot_partner(wq3[..., QK_NOPE:])).reshape(q_lora, n_heads * LANES).astype(BF16)
        wkv3 = w_kv_up[i].reshape(kv_lora, n_heads, QK_NOPE + V_HEAD)
        wkn = _head_tiles(wkv3[..., :QK_NOPE].reshape(kv_lora, -1), QK_NOPE).astype(BF16)
        wv = wkv3[..., QK_NOPE:].reshape(kv_lora, n_heads * V_HEAD).astype(BF16)
        gq, gk = g_q_head[i], g_k_head[i]
        gains = jnp.stack([
            jnp.pad(gq, (0, LANES - QK_HEAD)),
            _rope_tile(jnp.concatenate([gq[QK_NOPE + ROPE_HALF:], gq[QK_NOPE:QK_NOPE + ROPE_HALF]])),
            jnp.pad(gk[:QK_NOPE], (0, LANES - QK_NOPE)),
            _rope_tile(gk[QK_NOPE:]),
            _rope_tile(jnp.concatenate([gk[QK_NOPE + ROPE_HALF:], gk[QK_NOPE:QK_NOPE + ROPE_HALF]])),
        ])
        gains = jnp.pad(gains, ((0, 3), (0, 0)))

        a_glu, q, k, v = _mixer_in(h, mod, cos_f, sin_f, w_in_r, g_q_lat[i].reshape(1, -1),
                                   g_kv_lat[i].reshape(1, -1), wq, wqp, wkn, wv, gains, tm,
                                   c_conv, q_lora, kv_lora, n_heads)
        a_conv = _conv_group(a_glu, conv_w[i].reshape(-1, c_conv), conv_b[i].reshape(1, -1),
                             g_conv_ln[i].reshape(1, -1), b_conv_ln[i].reshape(1, -1))
        att = _attention(q, k, v, tq=512)
        h = _ffn2(h, a_conv, att, mod, w_out[i].astype(BF16), ffn2_wg[i].astype(BF16),
                  ffn2_wu[i].astype(BF16), ffn2_wd[i].astype(BF16), g_final[i].reshape(1, -1), tm)
    return h
```

```python
import functools

import jax
import jax.numpy as jnp
from jax import lax
from jax.experimental import pallas as pl
from jax.experimental.pallas import tpu as pltpu

F32 = jnp.float32
BF16 = jnp.bfloat16

LANES = 128
EPS = 1e-6
LN_EPS = 1e-5
ROPE_THETA = 10000.0
V_HEAD = 64
QK_NOPE = 64
QK_ROPE = 32
QK_HEAD = QK_NOPE + QK_ROPE
ROPE_HALF = QK_ROPE // 2
Q_SCALE = QK_HEAD ** -0.5 * 1.4426950408889634
MXU_WIDTH = 256
PV_GROUP = MXU_WIDTH // V_HEAD
PV_WIDTH = PV_GROUP * V_HEAD
CONV_HALO = 16

VMEM_LIMIT = 56 * 1024 * 1024


def _const_spec(shape):
    zeros = (0,) * len(shape)
    return pl.BlockSpec(shape, lambda *_: zeros, pipeline_mode=pl.Buffered(1))


def _sigmoid(x):
    return 1.0 / (1.0 + jnp.exp(-x))


def _rms(x):
    return x * lax.rsqrt(jnp.mean(x * x, axis=-1, keepdims=True) + EPS)


def _rope_table_kernel(pos_ref, inv_ref, cos_ref, sin_ref):
    ang = inv_ref[...] * pos_ref[...].astype(F32)
    cos_ref[...] = jnp.cos(ang)
    sin_ref[...] = jnp.sin(ang)


def _rope_tables(positions):
    B, S = positions.shape
    n = B * S
    blk = 4096
    inv_freq = 1.0 / (ROPE_THETA ** (jnp.arange(0, QK_ROPE, 2, dtype=F32) / QK_ROPE))
    cos_t, sin_t = pl.pallas_call(
        _rope_table_kernel,
        out_shape=(jax.ShapeDtypeStruct((ROPE_HALF, n), F32),) * 2,
        grid=(n // blk,),
        in_specs=[pl.BlockSpec((1, blk), lambda i: (0, i)),
                  pl.BlockSpec((ROPE_HALF, 1), lambda i: (0, 0))],
        out_specs=(pl.BlockSpec((ROPE_HALF, blk), lambda i: (0, i)),) * 2,
        name="rope_table",
    )(positions.reshape(1, n), inv_freq.reshape(ROPE_HALF, 1))
    cos = cos_t.T.reshape(B, S, ROPE_HALF)
    sin = sin_t.T.reshape(B, S, ROPE_HALF)
    ones = jnp.ones((B, S, QK_NOPE), F32)
    zeros_n = jnp.zeros((B, S, QK_NOPE), F32)
    zeros_p = jnp.zeros((B, S, LANES - QK_HEAD), F32)
    cos_f = jnp.concatenate([ones, cos, cos, zeros_p], axis=-1)
    sin_f = jnp.concatenate([zeros_n, sin, sin, zeros_p], axis=-1)
    return cos_f, sin_f


def _ada_kernel(c_ref, w_ref, b_ref, o_ref):
    c = c_ref[...]
    sc = (c * _sigmoid(c)).astype(BF16)
    o_ref[0] = jnp.dot(sc, w_ref[...].astype(BF16), preferred_element_type=F32) + b_ref[...]


def _ada_mod(c, w_ada, b_ada):
    B, D = c.shape
    n_mod = w_ada.shape[1] // D
    mod = pl.pallas_call(
        _ada_kernel,
        out_shape=jax.ShapeDtypeStruct((n_mod, B, D), F32),
        grid=(n_mod,),
        in_specs=[pl.BlockSpec((B, D), lambda j: (0, 0)),
                  pl.BlockSpec((D, D), lambda j: (0, j)),
                  pl.BlockSpec((1, D), lambda j: (0, j))],
        out_specs=pl.BlockSpec((1, B, D), lambda j: (j, 0, 0)),
        name="ada_mod",
    )(c, w_ada, b_ada.reshape(1, n_mod * D))
    return mod.reshape(n_mod, B, 1, D)


def _ffn_body(x, mod_ref, rows, wg_ref, wu_ref, wd_ref):
    shift, scale, gate = (mod_ref[r, 0] for r in rows)
    n = (_rms(x) * (1.0 + scale) + shift).astype(BF16)
    g = jnp.dot(n, wg_ref[...], preferred_element_type=F32)
    u = jnp.dot(n, wu_ref[...], preferred_element_type=F32)
    a = (g * _sigmoid(g) * u).astype(BF16)
    y = jnp.dot(a, wd_ref[...], preferred_element_type=F32)
    return x + (0.5 * gate) * y


def _ffn1_kernel(h_ref, mod_ref, wg_ref, wu_ref, wd_ref, o_ref):
    o_ref[0] = _ffn_body(h_ref[0], mod_ref, (0, 1, 2), wg_ref, wu_ref, wd_ref)


def _conv_tile(pad_ref, shift_ref, w_ref, b_ref, g_ref, beta_ref, out_ref, *, conv_k, rows):
    C = pad_ref.shape[1]
    tm = out_ref.shape[0]
    first = CONV_HALO - conv_k // 2
    span = rows + 2 * CONV_HALO - 8
    for ci in range(tm // rows):
        base = ci * rows
        cols = []
        for lt in range(C // LANES):
            ls = slice(lt * LANES, (lt + 1) * LANES)
            win = pad_ref[base:base + rows + 2 * CONV_HALO, ls]
            for r in range(1, 8):
                shift_ref[ci % 2, r, :, ls] = win[r:r + span, :]
            acc = jnp.zeros((rows, LANES), F32) + b_ref[:, ls]
            for t in range(conv_k):
                r = (first + t) % 8
                q8 = first + t - r
                tap = win[q8:q8 + rows, :] if r == 0 else shift_ref[ci % 2, r, q8:q8 + rows, ls]
                acc += tap * w_ref[t:t + 1, ls]
            cols.append(acc)
        acc = jnp.concatenate(cols, axis=-1)
        mu = jnp.mean(acc, axis=-1, keepdims=True)
        d = acc - mu
        var = jnp.mean(d * d, axis=-1, keepdims=True)
        y = d * lax.rsqrt(var + LN_EPS) * g_ref[...] + beta_ref[...]
        out_ref[base:base + rows, :] = (y * _sigmoid(y)).astype(out_ref.dtype)


def _ffn2_kernel(h_ref, att_ref, mod_ref, a_ref, a_prev_ref, a_next_ref, wo_ref, wg_ref, wu_ref,
                 wd_ref, gf_ref, cw_ref, cb_ref, lg_ref, lb_ref, o_ref,
                 aconv_ref, pad_ref, shift_ref, *, n_tiles, n_steps, conv_k, rows):
    g = pl.program_id(0)
    tm, c_conv = aconv_ref.shape

    @pl.when(g == 0)
    def _():
        aconv_ref[...] = jnp.zeros_like(aconv_ref)

    a_conv = aconv_ref[...]

    tile = jnp.minimum(g, n_steps - 1) % n_tiles
    pad_ref[:CONV_HALO, :] = jnp.where(tile > 0, a_prev_ref[0], 0.0)
    pad_ref[CONV_HALO:CONV_HALO + tm, :] = a_ref[0]
    pad_ref[CONV_HALO + tm:, :] = jnp.where(tile < n_tiles - 1, a_next_ref[0], 0.0)
    _conv_tile(pad_ref, shift_ref, cw_ref, cb_ref, lg_ref, lb_ref, aconv_ref, conv_k=conv_k, rows=rows)

    mix = jnp.dot(a_conv, wo_ref[:c_conv, :], preferred_element_type=F32)
    mix += jnp.dot(att_ref[0], wo_ref[c_conv:, :], preferred_element_type=F32)
    x = h_ref[0] + mod_ref[5, 0] * mix
    out = _ffn_body(x, mod_ref, (6, 7, 8), wg_ref, wu_ref, wd_ref)
    o_ref[0] = _rms(out) * gf_ref[...]


def _mod_spec(n_mod, D):
    return pl.BlockSpec((n_mod, 1, 1, D), lambda b, i: (0, b, 0, 0))


def _ffn1(h, mod, wg, wu, wd, tm):
    B, S, D = h.shape
    F = wg.shape[1]
    tok = pl.BlockSpec((1, tm, D), lambda b, i: (b, i, 0))
    return pl.pallas_call(
        _ffn1_kernel,
        out_shape=jax.ShapeDtypeStruct((B, S, D), F32),
        grid=(B, S // tm),
        in_specs=[tok, _mod_spec(mod.shape[0], D),
                  _const_spec((D, F)), _const_spec((D, F)), _const_spec((F, D))],
        out_specs=tok,
        compiler_params=pltpu.CompilerParams(
            dimension_semantics=("parallel", "parallel"), vmem_limit_bytes=VMEM_LIMIT),
        name="ffn1",
    )(h, mod, wg, wu, wd)


def _ffn2(h, a_glu, att, mod, wo, wg, wu, wd, g_final, conv_w, conv_b, g_ln, b_ln, tm, rows=64):
    B, S, D = h.shape
    F = wg.shape[1]
    C = a_glu.shape[-1]
    conv_k = conv_w.shape[0]
    nt = S // tm
    n_steps = B * nt
    hb = tm // CONV_HALO
    cur = lambda g: jnp.maximum(g - 1, 0)
    nxt = lambda g: jnp.minimum(g, n_steps - 1)
    tok = lambda w: pl.BlockSpec((1, tm, w), lambda g: (cur(g) // nt, cur(g) % nt, 0))
    kern = functools.partial(_ffn2_kernel, n_tiles=nt, n_steps=n_steps, conv_k=conv_k, rows=rows)
    return pl.pallas_call(
        kern,
        out_shape=jax.ShapeDtypeStruct((B, S, D), F32),
        grid=(n_steps + 1,),
        in_specs=[tok(D), tok(C),
                  pl.BlockSpec((mod.shape[0], 1, 1, D), lambda g: (0, cur(g) // nt, 0, 0)),
                  pl.BlockSpec((1, tm, C), lambda g: (nxt(g) // nt, nxt(g) % nt, 0)),
                  pl.BlockSpec((1, CONV_HALO, C), lambda g: (
                      nxt(g) // nt, jnp.maximum((nxt(g) % nt) * hb - 1, 0), 0)),
                  pl.BlockSpec((1, CONV_HALO, C), lambda g: (
                      nxt(g) // nt, jnp.minimum((nxt(g) % nt + 1) * hb, S // CONV_HALO - 1), 0)),
                  _const_spec((D, D)), _const_spec((D, F)), _const_spec((D, F)), _const_spec((F, D)),
                  _const_spec((1, D)), _const_spec((conv_k, C)), _const_spec((1, C)),
                  _const_spec((1, C)), _const_spec((1, C))],
        out_specs=tok(D),
        scratch_shapes=[pltpu.VMEM((tm, C), BF16),
                        pltpu.VMEM((tm + 2 * CONV_HALO, C), F32),
                        pltpu.VMEM((2, 8, rows + 2 * CONV_HALO - 8, C), F32)],
        compiler_params=pltpu.CompilerParams(
            dimension_semantics=("arbitrary",), vmem_limit_bytes=VMEM_LIMIT),
        name="conv_mix_ffn2_norm",
    )(h, att, mod, a_glu, a_glu, a_glu, wo, wg, wu, wd, g_final, conv_w, conv_b, g_ln, b_ln)


def _mixer_in_kernel(h_ref, mod_ref, cos_ref, sin_ref, w_in_ref, gql_ref, gkvl_ref,
                     wq_ref, wqp_ref, wkn_ref, wv_ref, gains_ref,
                     a_ref, q_ref, k_ref, v_ref, *, c_conv, q_lora, kv_lora, n_heads):
    x = h_ref[0]
    n = (_rms(x) * (1.0 + mod_ref[4, 0]) + mod_ref[3, 0]).astype(BF16)
    z = jnp.dot(n, w_in_ref[...], preferred_element_type=F32)
    o1 = 2 * c_conv
    o2 = o1 + q_lora
    o3 = o2 + kv_lora
    a_ref[0] = z[:, :c_conv] * _sigmoid(z[:, c_conv:o1])
    cqn = (_rms(z[:, o1:o2]) * gql_ref[...]).astype(BF16)
    ckvn = (_rms(z[:, o2:o3]) * gkvl_ref[...]).astype(BF16)
    kr = z[:, o3:o3 + LANES]
    krp = z[:, o3 + LANES:o3 + 2 * LANES]
    q = jnp.dot(cqn, wq_ref[...], preferred_element_type=F32)
    qp = jnp.dot(cqn, wqp_ref[...], preferred_element_type=F32)
    kn = jnp.dot(ckvn, wkn_ref[...], preferred_element_type=F32)
    v_ref[0] = jnp.dot(ckvn, wv_ref[...], preferred_element_type=F32).astype(BF16)

    cos_f = cos_ref[0]
    sin_f = sin_ref[0]
    gq, gqp, gkn, gkr, gkrp = (gains_ref[i:i + 1, :] for i in range(5))
    kr_rot = kr * gkr * cos_f + krp * gkrp * sin_f
    kr_ssq = jnp.sum(kr * kr, axis=-1, keepdims=True)
    q_cos = gq * Q_SCALE * cos_f
    q_sin = gqp * Q_SCALE * sin_f
    inv_d = 1.0 / QK_HEAD
    for hd in range(n_heads):
        sl = slice(hd * LANES, (hd + 1) * LANES)
        qh = q[:, sl]
        rq = lax.rsqrt(jnp.sum(qh * qh, axis=-1, keepdims=True) * inv_d + EPS)
        q_ref[0, :, sl] = ((qh * q_cos + qp[:, sl] * q_sin) * rq).astype(BF16)
        kh = kn[:, sl]
        rk = lax.rsqrt((jnp.sum(kh * kh, axis=-1, keepdims=True) + kr_ssq) * inv_d + EPS)
        k_ref[0, :, sl] = ((kh * gkn + kr_rot) * rk).astype(BF16)


def _mixer_in(h, mod, cos_f, sin_f, w_in_r, gql, gkvl, wq, wqp, wkn, wv, gains, tm,
              c_conv, q_lora, kv_lora, n_heads):
    B, S, D = h.shape
    tok = lambda w: pl.BlockSpec((1, tm, w), lambda b, i: (b, i, 0))
    hw = n_heads * LANES
    kern = functools.partial(_mixer_in_kernel, c_conv=c_conv, q_lora=q_lora,
                             kv_lora=kv_lora, n_heads=n_heads)
    return pl.pallas_call(
        kern,
        out_shape=(jax.ShapeDtypeStruct((B, S, c_conv), F32),
                   jax.ShapeDtypeStruct((B, S, hw), BF16),
                   jax.ShapeDtypeStruct((B, S, hw), BF16),
                   jax.ShapeDtypeStruct((B, S, n_heads * V_HEAD), BF16)),
        grid=(B, S // tm),
        in_specs=[tok(D), _mod_spec(mod.shape[0], D), tok(LANES), tok(LANES),
                  _const_spec(w_in_r.shape), _const_spec(gql.shape), _const_spec(gkvl.shape),
                  _const_spec(wq.shape), _const_spec(wqp.shape), _const_spec(wkn.shape),
                  _const_spec(wv.shape), _const_spec(gains.shape)],
        out_specs=(tok(c_conv), tok(hw), tok(hw), tok(n_heads * V_HEAD)),
        compiler_params=pltpu.CompilerParams(
            dimension_semantics=("parallel", "parallel"), vmem_limit_bytes=VMEM_LIMIT),
        name="mixer_in",
    )(h, mod, cos_f, sin_f, w_in_r, gql, gkvl, wq, wqp, wkn, wv, gains)


def _attn_kernel(q_ref, k_ref, v_ref, o_ref, *, n_heads):
    def scores(hd):
        sl = slice(hd * LANES, (hd + 1) * LANES)
        return lax.dot_general(q_ref[0, :, sl], k_ref[0, :, sl], (((1,), (1,)), ((), ())),
                               preferred_element_type=F32)

    def attend(hd, s):
        tiles = [s[:, c * LANES:(c + 1) * LANES] for c in range(s.shape[1] // LANES)]
        m_part = tiles[0]
        for t in tiles[1:]:
            m_part = jnp.maximum(m_part, t)
        m = jnp.max(m_part, axis=-1, keepdims=True)
        l_part = jnp.zeros_like(m_part)
        ps = []
        for t in tiles:
            pc = jnp.exp2(t - m)
            l_part += pc
            ps.append(pc.astype(BF16))
        l = jnp.sum(l_part, axis=-1, keepdims=True)
        grp = slice((hd // PV_GROUP) * PV_WIDTH, (hd // PV_GROUP + 1) * PV_WIDTH)
        o = jnp.dot(jnp.concatenate(ps, axis=-1), v_ref[0, :, grp], preferred_element_type=F32)
        return o / l

    s_next = scores(0)
    outs = []
    for hd in range(n_heads):
        s = s_next
        if hd + 1 < n_heads:
            s_next = scores(hd + 1)
        outs.append(attend(hd, s))
    slot = lax.broadcasted_iota(jnp.int32, outs[0].shape, 1) // V_HEAD
    for g in range(n_heads // PV_GROUP):
        o = outs[g * PV_GROUP]
        for j in range(1, PV_GROUP):
            o = jnp.where(slot == j, outs[g * PV_GROUP + j], o)
        o_ref[0, :, g * PV_WIDTH:(g + 1) * PV_WIDTH] = o.astype(o_ref.dtype)


def _attention(q, k, v, tq):
    B, S, hw = q.shape
    vw = v.shape[-1]
    n_heads = hw // LANES
    return pl.pallas_call(
        functools.partial(_attn_kernel, n_heads=n_heads),
        out_shape=jax.ShapeDtypeStruct(v.shape, BF16),
        grid=(B, S // tq),
        in_specs=[pl.BlockSpec((1, tq, hw), lambda b, i: (b, i, 0)),
                  pl.BlockSpec((1, S, hw), lambda b, i: (b, 0, 0)),
                  pl.BlockSpec((1, S, vw), lambda b, i: (b, 0, 0))],
        out_specs=pl.BlockSpec((1, tq, vw), lambda b, i: (b, i, 0)),
        compiler_params=pltpu.CompilerParams(
            dimension_semantics=("parallel", "parallel"), vmem_limit_bytes=VMEM_LIMIT),
        name="attention",
    )(q, k, v)


def _head_tiles(w, width):
    lead = w.shape[:-1]
    heads = w.shape[-1] // width
    w = w.reshape(*lead, heads, width)
    w = jnp.pad(w, [(0, 0)] * len(lead) + [(0, 0), (0, LANES - width)])
    return w.reshape(*lead, heads * LANES)


def _rot_partner(w):
    return jnp.concatenate([-w[..., ROPE_HALF:], w[..., :ROPE_HALF]], axis=-1)


def _rope_tile(w):
    lead = [(0, 0)] * (w.ndim - 1)
    return jnp.pad(w, lead + [(QK_NOPE, LANES - QK_HEAD)])


def kernel(x, c, positions, w_ada, b_ada, ffn1_wg, ffn1_wu, ffn1_wd, w_in, g_q_lat, g_kv_lat,
           w_q_up, w_kv_up, g_q_head, g_k_head, conv_w, conv_b, g_conv_ln, b_conv_ln, w_out,
           ffn2_wg, ffn2_wu, ffn2_wd, g_final):
    B, S, D = x.shape
    depth = w_ada.shape[0]
    c_conv = conv_b.shape[-1]
    q_lora = g_q_lat.shape[-1]
    kv_lora = g_kv_lat.shape[-1]
    n_heads = w_q_up.shape[-1] // QK_HEAD
    tm = 512

    cos_f, sin_f = _rope_tables(positions)
    h = x
    for i in range(depth):
        mod = _ada_mod(c, w_ada[i], b_ada[i])
        h = _ffn1(h, mod, ffn1_wg[i].astype(BF16), ffn1_wu[i].astype(BF16),
                  ffn1_wd[i].astype(BF16), tm)

        o3 = 2 * c_conv + q_lora + kv_lora
        w_kr = w_in[i][:, o3:]
        w_in_r = jnp.concatenate(
            [w_in[i][:, :o3], _rope_tile(w_kr), _rope_tile(_rot_partner(w_kr))], axis=-1).astype(BF16)
        wq3 = w_q_up[i].reshape(q_lora, n_heads, QK_HEAD)
        wq = _head_tiles(w_q_up[i], QK_HEAD).astype(BF16)
        wqp = _rope_tile(_rot_partner(wq3[..., QK_NOPE:])).reshape(q_lora, n_heads * LANES).astype(BF16)
        wkv3 = w_kv_up[i].reshape(kv_lora, n_heads, QK_NOPE + V_HEAD)
        wkn = _head_tiles(wkv3[..., :QK_NOPE].reshape(kv_lora, -1), QK_NOPE).astype(BF16)
        wv = wkv3[..., QK_NOPE:].reshape(kv_lora, n_heads * V_HEAD).astype(BF16)
        gq, gk = g_q_head[i], g_k_head[i]
        gains = jnp.stack([
            jnp.pad(gq, (0, LANES - QK_HEAD)),
            _rope_tile(jnp.concatenate([gq[QK_NOPE + ROPE_HALF:], gq[QK_NOPE:QK_NOPE + ROPE_HALF]])),
            jnp.pad(gk[:QK_NOPE], (0, LANES - QK_NOPE)),
            _rope_tile(gk[QK_NOPE:]),
            _rope_tile(jnp.concatenate([gk[QK_NOPE + ROPE_HALF:], gk[QK_NOPE:QK_NOPE + ROPE_HALF]])),
        ])
        gains = jnp.pad(gains, ((0, 3), (0, 0)))

        a_glu, q, k, v = _mixer_in(h, mod, cos_f, sin_f, w_in_r, g_q_lat[i].reshape(1, -1),
                                   g_kv_lat[i].reshape(1, -1), wq, wqp, wkn, wv, gains, tm,
                                   c_conv, q_lora, kv_lora, n_heads)
        att = _attention(q, k, v, tq=512)
        h = _ffn2(h, a_glu, att, mod, w_out[i].astype(BF16), ffn2_wg[i].astype(BF16),
                  ffn2_wu[i].astype(BF16), ffn2_wd[i].astype(BF16), g_final[i].reshape(1, -1),
                  conv_w[i].reshape(-1, c_conv), conv_b[i].reshape(1, -1),
                  g_conv_ln[i].reshape(1, -1), b_conv_ln[i].reshape(1, -1), tm)
    return h
```

```python
import functools

import jax
import jax.numpy as jnp
from jax import lax
from jax.experimental import pallas as pl
from jax.experimental.pallas import tpu as pltpu

F32 = jnp.float32
BF16 = jnp.bfloat16

LANES = 128
EPS = 1e-6
LN_EPS = 1e-5
ROPE_THETA = 10000.0
V_HEAD = 64
QK_NOPE = 64
QK_ROPE = 32
QK_HEAD = QK_NOPE + QK_ROPE
ROPE_HALF = QK_ROPE // 2
Q_SCALE = QK_HEAD ** -0.5 * 1.4426950408889634
MXU_WIDTH = 256
PV_GROUP = MXU_WIDTH // V_HEAD
PV_WIDTH = PV_GROUP * V_HEAD
CONV_HALO = 16

VMEM_LIMIT = 56 * 1024 * 1024


def _const_spec(shape):
    zeros = (0,) * len(shape)
    return pl.BlockSpec(shape, lambda *_: zeros, pipeline_mode=pl.Buffered(1))


def _sigmoid(x):
    return 1.0 / (1.0 + jnp.exp(-x))


def _rms(x):
    return x * lax.rsqrt(jnp.mean(x * x, axis=-1, keepdims=True) + EPS)


def _packed_angles(positions):
    B, S = positions.shape
    inv_freq = 1.0 / (ROPE_THETA ** (jnp.arange(0, QK_ROPE, 2, dtype=F32) / QK_ROPE))
    ang = positions.astype(F32)[..., None] * inv_freq
    ang = ang.reshape(B, S // 8, 8, ROPE_HALF)
    order = [(4 - g) % 8 for g in range(8)]
    return ang[:, :, order, :].reshape(B, S // 8, 8 * ROPE_HALF)


def _ada_kernel(c_ref, w_ref, b_ref, o_ref):
    c = c_ref[...]
    sc = (c * _sigmoid(c)).astype(BF16)
    o_ref[0] = jnp.dot(sc, w_ref[...].astype(BF16), preferred_element_type=F32) + b_ref[...]


def _ada_mod(c, w_ada, b_ada):
    B, D = c.shape
    n_mod = w_ada.shape[1] // D
    mod = pl.pallas_call(
        _ada_kernel,
        out_shape=jax.ShapeDtypeStruct((n_mod, B, D), F32),
        grid=(n_mod,),
        in_specs=[pl.BlockSpec((B, D), lambda j: (0, 0)),
                  pl.BlockSpec((D, D), lambda j: (0, j)),
                  pl.BlockSpec((1, D), lambda j: (0, j))],
        out_specs=pl.BlockSpec((1, B, D), lambda j: (j, 0, 0)),
        name="ada_mod",
    )(c, w_ada, b_ada.reshape(1, n_mod * D))
    return mod.reshape(n_mod, B, 1, D)


def _ffn_body(x, mod_ref, rows, wg_ref, wu_ref, wd_ref):
    shift, scale, gate = (mod_ref[r, 0] for r in rows)
    n = (_rms(x) * (1.0 + scale) + shift).astype(BF16)
    g = jnp.dot(n, wg_ref[...], preferred_element_type=F32)
    u = jnp.dot(n, wu_ref[...], preferred_element_type=F32)
    a = (g * _sigmoid(g) * u).astype(BF16)
    y = jnp.dot(a, wd_ref[...], preferred_element_type=F32)
    return x + (0.5 * gate) * y


def _ffn1_kernel(h_ref, mod_ref, wg_ref, wu_ref, wd_ref, o_ref):
    o_ref[0] = _ffn_body(h_ref[0], mod_ref, (0, 1, 2), wg_ref, wu_ref, wd_ref)


def _conv_tile(pad_ref, shift_ref, w_ref, b_ref, g_ref, beta_ref, out_ref, *, conv_k, rows):
    C = pad_ref.shape[1]
    tm = out_ref.shape[0]
    first = CONV_HALO - conv_k // 2
    span = rows + 2 * CONV_HALO - 8
    for ci in range(tm // rows):
        base = ci * rows
        cols = []
        for lt in range(C // LANES):
            ls = slice(lt * LANES, (lt + 1) * LANES)
            win = pad_ref[base:base + rows + 2 * CONV_HALO, ls]
            for r in range(1, 8):
                shift_ref[ci % 2, r, :, ls] = win[r:r + span, :]
            acc = jnp.zeros((rows, LANES), F32) + b_ref[:, ls]
            for t in range(conv_k):
                r = (first + t) % 8
                q8 = first + t - r
                tap = win[q8:q8 + rows, :] if r == 0 else shift_ref[ci % 2, r, q8:q8 + rows, ls]
                acc += tap * w_ref[t:t + 1, ls]
            cols.append(acc)
        acc = jnp.concatenate(cols, axis=-1)
        mu = jnp.mean(acc, axis=-1, keepdims=True)
        d = acc - mu
        var = jnp.mean(d * d, axis=-1, keepdims=True)
        y = d * lax.rsqrt(var + LN_EPS) * g_ref[...] + beta_ref[...]
        out_ref[base:base + rows, :] = (y * _sigmoid(y)).astype(out_ref.dtype)


def _ffn2_kernel(h_ref, att_ref, mod_ref, a_ref, a_prev_ref, a_next_ref, wo_ref, wg_ref, wu_ref,
                 wd_ref, gf_ref, cw_ref, cb_ref, lg_ref, lb_ref, o_ref,
                 aconv_ref, pad_ref, shift_ref, *, n_tiles, n_steps, conv_k, rows):
    g = pl.program_id(0)
    tm, c_conv = aconv_ref.shape

    @pl.when(g == 0)
    def _():
        aconv_ref[...] = jnp.zeros_like(aconv_ref)

    a_conv = aconv_ref[...]

    tile = jnp.minimum(g, n_steps - 1) % n_tiles
    pad_ref[:CONV_HALO, :] = jnp.where(tile > 0, a_prev_ref[0], 0.0)
    pad_ref[CONV_HALO:CONV_HALO + tm, :] = a_ref[0]
    pad_ref[CONV_HALO + tm:, :] = jnp.where(tile < n_tiles - 1, a_next_ref[0], 0.0)
    _conv_tile(pad_ref, shift_ref, cw_ref, cb_ref, lg_ref, lb_ref, aconv_ref, conv_k=conv_k, rows=rows)

    mix = jnp.dot(a_conv, wo_ref[:c_conv, :], preferred_element_type=F32)
    mix += jnp.dot(att_ref[0], wo_ref[c_conv:, :], preferred_element_type=F32)
    x = h_ref[0] + mod_ref[5, 0] * mix
    out = _ffn_body(x, mod_ref, (6, 7, 8), wg_ref, wu_ref, wd_ref)
    o_ref[0] = _rms(out) * gf_ref[...]


def _mod_spec(n_mod, D):
    return pl.BlockSpec((n_mod, 1, 1, D), lambda b, i: (0, b, 0, 0))


def _ffn1(h, mod, wg, wu, wd, tm):
    B, S, D = h.shape
    F = wg.shape[1]
    tok = pl.BlockSpec((1, tm, D), lambda b, i: (b, i, 0))
    return pl.pallas_call(
        _ffn1_kernel,
        out_shape=jax.ShapeDtypeStruct((B, S, D), F32),
        grid=(B, S // tm),
        in_specs=[tok, _mod_spec(mod.shape[0], D),
                  _const_spec((D, F)), _const_spec((D, F)), _const_spec((F, D))],
        out_specs=tok,
        compiler_params=pltpu.CompilerParams(
            dimension_semantics=("parallel", "parallel"), vmem_limit_bytes=VMEM_LIMIT),
        name="ffn1",
    )(h, mod, wg, wu, wd)


def _ffn2(h, a_glu, att, mod, wo, wg, wu, wd, g_final, conv_w, conv_b, g_ln, b_ln, tm, rows=64):
    B, S, D = h.shape
    F = wg.shape[1]
    C = a_glu.shape[-1]
    conv_k = conv_w.shape[0]
    nt = S // tm
    n_steps = B * nt
    hb = tm // CONV_HALO
    cur = lambda g: jnp.maximum(g - 1, 0)
    nxt = lambda g: jnp.minimum(g, n_steps - 1)
    tok = lambda w: pl.BlockSpec((1, tm, w), lambda g: (cur(g) // nt, cur(g) % nt, 0))
    kern = functools.partial(_ffn2_kernel, n_tiles=nt, n_steps=n_steps, conv_k=conv_k, rows=rows)
    return pl.pallas_call(
        kern,
        out_shape=jax.ShapeDtypeStruct((B, S, D), F32),
        grid=(n_steps + 1,),
        in_specs=[tok(D), tok(C),
                  pl.BlockSpec((mod.shape[0], 1, 1, D), lambda g: (0, cur(g) // nt, 0, 0)),
                  pl.BlockSpec((1, tm, C), lambda g: (nxt(g) // nt, nxt(g) % nt, 0)),
                  pl.BlockSpec((1, CONV_HALO, C), lambda g: (
                      nxt(g) // nt, jnp.maximum((nxt(g) % nt) * hb - 1, 0), 0)),
                  pl.BlockSpec((1, CONV_HALO, C), lambda g: (
                      nxt(g) // nt, jnp.minimum((nxt(g) % nt + 1) * hb, S // CONV_HALO - 1), 0)),
                  _const_spec((D, D)), _const_spec((D, F)), _const_spec((D, F)), _const_spec((F, D)),
                  _const_spec((1, D)), _const_spec((conv_k, C)), _const_spec((1, C)),
                  _const_spec((1, C)), _const_spec((1, C))],
        out_specs=tok(D),
        scratch_shapes=[pltpu.VMEM((tm, C), BF16),
                        pltpu.VMEM((tm + 2 * CONV_HALO, C), F32),
                        pltpu.VMEM((2, 8, rows + 2 * CONV_HALO - 8, C), F32)],
        compiler_params=pltpu.CompilerParams(
            dimension_semantics=("arbitrary",), vmem_limit_bytes=VMEM_LIMIT),
        name="conv_mix_ffn2_norm",
    )(h, att, mod, a_glu, a_glu, a_glu, wo, wg, wu, wd, g_final, conv_w, conv_b, g_ln, b_ln)


def _rope_factors(ang8):
    rows = ang8.shape[0]
    lane = lax.broadcasted_iota(jnp.int32, (rows * 8, LANES), 1)
    lo = (lane >= QK_NOPE) & (lane < QK_NOPE + ROPE_HALF)
    hi = (lane >= QK_NOPE + ROPE_HALF) & (lane < QK_HEAD)

    def spread(t, fill):
        rep = jnp.broadcast_to(t[:, None, :], (rows, 8, LANES)).reshape(rows * 8, LANES)
        at_lo = pltpu.roll(rep, 0, 1, stride=ROPE_HALF, stride_axis=0)
        at_hi = pltpu.roll(rep, ROPE_HALF, 1, stride=ROPE_HALF, stride_axis=0)
        return jnp.where(lo, at_lo, jnp.where(hi, at_hi, fill))

    ones_nope = jnp.where(lane < QK_NOPE, 1.0, 0.0)
    return spread(jnp.cos(ang8), ones_nope), spread(jnp.sin(ang8), 0.0)


def _mixer_in_kernel(h_ref, mod_ref, ang_ref, w_in_ref, gql_ref, gkvl_ref,
                     wq_ref, wqp_ref, wkn_ref, wv_ref, gains_ref,
                     a_ref, q_ref, k_ref, v_ref, *, c_conv, q_lora, kv_lora, n_heads):
    x = h_ref[0]
    n = (_rms(x) * (1.0 + mod_ref[4, 0]) + mod_ref[3, 0]).astype(BF16)
    z = jnp.dot(n, w_in_ref[...], preferred_element_type=F32)
    o1 = 2 * c_conv
    o2 = o1 + q_lora
    o3 = o2 + kv_lora
    a_ref[0] = z[:, :c_conv] * _sigmoid(z[:, c_conv:o1])
    cqn = (_rms(z[:, o1:o2]) * gql_ref[...]).astype(BF16)
    ckvn = (_rms(z[:, o2:o3]) * gkvl_ref[...]).astype(BF16)
    kr = z[:, o3:o3 + LANES]
    krp = z[:, o3 + LANES:o3 + 2 * LANES]
    q = jnp.dot(cqn, wq_ref[...], preferred_element_type=F32)
    qp = jnp.dot(cqn, wqp_ref[...], preferred_element_type=F32)
    kn = jnp.dot(ckvn, wkn_ref[...], preferred_element_type=F32)
    v_ref[0] = jnp.dot(ckvn, wv_ref[...], preferred_element_type=F32).astype(BF16)

    cos_f, sin_f = _rope_factors(ang_ref[0])
    gq, gqp, gkn, gkr, gkrp = (gains_ref[i:i + 1, :] for i in range(5))
    kr_rot = kr * gkr * cos_f + krp * gkrp * sin_f
    kr_ssq = jnp.sum(kr * kr, axis=-1, keepdims=True)
    q_cos = gq * Q_SCALE * cos_f
    q_sin = gqp * Q_SCALE * sin_f
    inv_d = 1.0 / QK_HEAD
    for hd in range(n_heads):
        sl = slice(hd * LANES, (hd + 1) * LANES)
        qh = q[:, sl]
        rq = lax.rsqrt(jnp.sum(qh * qh, axis=-1, keepdims=True) * inv_d + EPS)
        q_ref[0, :, sl] = ((qh * q_cos + qp[:, sl] * q_sin) * rq).astype(BF16)
        kh = kn[:, sl]
        rk = lax.rsqrt((jnp.sum(kh * kh, axis=-1, keepdims=True) + kr_ssq) * inv_d + EPS)
        k_ref[0, :, sl] = ((kh * gkn + kr_rot) * rk).astype(BF16)


def _mixer_in(h, mod, ang8, w_in_r, gql, gkvl, wq, wqp, wkn, wv, gains, tm,
              c_conv, q_lora, kv_lora, n_heads):
    B, S, D = h.shape
    tok = lambda w: pl.BlockSpec((1, tm, w), lambda b, i: (b, i, 0))
    hw = n_heads * LANES
    kern = functools.partial(_mixer_in_kernel, c_conv=c_conv, q_lora=q_lora,
                             kv_lora=kv_lora, n_heads=n_heads)
    return pl.pallas_call(
        kern,
        out_shape=(jax.ShapeDtypeStruct((B, S, c_conv), F32),
                   jax.ShapeDtypeStruct((B, S, hw), BF16),
                   jax.ShapeDtypeStruct((B, S, hw), BF16),
                   jax.ShapeDtypeStruct((B, S, n_heads * V_HEAD), BF16)),
        grid=(B, S // tm),
        in_specs=[tok(D), _mod_spec(mod.shape[0], D),
                  pl.BlockSpec((1, tm // 8, LANES), lambda b, i: (b, i, 0)),
                  _const_spec(w_in_r.shape), _const_spec(gql.shape), _const_spec(gkvl.shape),
                  _const_spec(wq.shape), _const_spec(wqp.shape), _const_spec(wkn.shape),
                  _const_spec(wv.shape), _const_spec(gains.shape)],
        out_specs=(tok(c_conv), tok(hw), tok(hw), tok(n_heads * V_HEAD)),
        compiler_params=pltpu.CompilerParams(
            dimension_semantics=("parallel", "parallel"), vmem_limit_bytes=VMEM_LIMIT),
        name="mixer_in",
    )(h, mod, ang8, w_in_r, gql, gkvl, wq, wqp, wkn, wv, gains)


def _attn_kernel(q_ref, k_ref, v_ref, o_ref, *, n_heads):
    def scores(hd):
        sl = slice(hd * LANES, (hd + 1) * LANES)
        return lax.dot_general(q_ref[0, :, sl], k_ref[0, :, sl], (((1,), (1,)), ((), ())),
                               preferred_element_type=F32)

    def attend(hd, s):
        tiles = [s[:, c * LANES:(c + 1) * LANES] for c in range(s.shape[1] // LANES)]
        m_part = tiles[0]
        for t in tiles[1:]:
            m_part = jnp.maximum(m_part, t)
        m = jnp.max(m_part, axis=-1, keepdims=True)
        l_part = jnp.zeros_like(m_part)
        ps = []
        for t in tiles:
            pc = jnp.exp2(t - m)
            l_part += pc
            ps.append(pc.astype(BF16))
        l = jnp.sum(l_part, axis=-1, keepdims=True)
        grp = slice((hd // PV_GROUP) * PV_WIDTH, (hd // PV_GROUP + 1) * PV_WIDTH)
        o = jnp.dot(jnp.concatenate(ps, axis=-1), v_ref[0, :, grp], preferred_element_type=F32)
        return o / l

    s_next = scores(0)
    outs = []
    for hd in range(n_heads):
        s = s_next
        if hd + 1 < n_heads:
            s_next = scores(hd + 1)
        outs.append(attend(hd, s))
    slot = lax.broadcasted_iota(jnp.int32, outs[0].shape, 1) // V_HEAD
    for g in range(n_heads // PV_GROUP):
        o = outs[g * PV_GROUP]
        for j in range(1, PV_GROUP):
            o = jnp.where(slot == j, outs[g * PV_GROUP + j], o)
        o_ref[0, :, g * PV_WIDTH:(g + 1) * PV_WIDTH] = o.astype(o_ref.dtype)


def _attention(q, k, v, tq):
    B, S, hw = q.shape
    vw = v.shape[-1]
    n_heads = hw // LANES
    return pl.pallas_call(
        functools.partial(_attn_kernel, n_heads=n_heads),
        out_shape=jax.ShapeDtypeStruct(v.shape, BF16),
        grid=(B, S // tq),
        in_specs=[pl.BlockSpec((1, tq, hw), lambda b, i: (b, i, 0)),
                  pl.BlockSpec((1, S, hw), lambda b, i: (b, 0, 0)),
                  pl.BlockSpec((1, S, vw), lambda b, i: (b, 0, 0))],
        out_specs=pl.BlockSpec((1, tq, vw), lambda b, i: (b, i, 0)),
        compiler_params=pltpu.CompilerParams(
            dimension_semantics=("parallel", "parallel"), vmem_limit_bytes=VMEM_LIMIT),
        name="attention",
    )(q, k, v)


def _head_tiles(w, width):
    lead = w.shape[:-1]
    heads = w.shape[-1] // width
    w = w.reshape(*lead, heads, width)
    w = jnp.pad(w, [(0, 0)] * len(lead) + [(0, 0), (0, LANES - width)])
    return w.reshape(*lead, heads * LANES)


def _rot_partner(w):
    return jnp.concatenate([-w[..., ROPE_HALF:], w[..., :ROPE_HALF]], axis=-1)


def _rope_tile(w):
    lead = [(0, 0)] * (w.ndim - 1)
    return jnp.pad(w, lead + [(QK_NOPE, LANES - QK_HEAD)])


def kernel(x, c, positions, w_ada, b_ada, ffn1_wg, ffn1_wu, ffn1_wd, w_in, g_q_lat, g_kv_lat,
           w_q_up, w_kv_up, g_q_head, g_k_head, conv_w, conv_b, g_conv_ln, b_conv_ln, w_out,
           ffn2_wg, ffn2_wu, ffn2_wd, g_final):
    B, S, D = x.shape
    depth = w_ada.shape[0]
    c_conv = conv_b.shape[-1]
    q_lora = g_q_lat.shape[-1]
    kv_lora = g_kv_lat.shape[-1]
    n_heads = w_q_up.shape[-1] // QK_HEAD
    tm = 512

    ang8 = _packed_angles(positions)
    h = x
    for i in range(depth):
        mod = _ada_mod(c, w_ada[i], b_ada[i])
        h = _ffn1(h, mod, ffn1_wg[i].astype(BF16), ffn1_wu[i].astype(BF16),
                  ffn1_wd[i].astype(BF16), tm)

        o3 = 2 * c_conv + q_lora + kv_lora
        w_kr = w_in[i][:, o3:]
        w_in_r = jnp.concatenate(
            [w_in[i][:, :o3], _rope_tile(w_kr), _rope_tile(_rot_partner(w_kr))], axis=-1).astype(BF16)
        wq3 = w_q_up[i].reshape(q_lora, n_heads, QK_HEAD)
        wq = _head_tiles(w_q_up[i], QK_HEAD).astype(BF16)
        wqp = _rope_tile(_rot_partner(wq3[..., QK_NOPE:])).reshape(q_lora, n_heads * LANES).astype(BF16)
        wkv3 = w_kv_up[i].reshape(kv_lora, n_heads, QK_NOPE + V_HEAD)
        wkn = _head_tiles(wkv3[..., :QK_NOPE].reshape(kv_lora, -1), QK_NOPE).astype(BF16)
        wv = wkv3[..., QK_NOPE:].reshape(kv_lora, n_heads * V_HEAD).astype(BF16)
        gq, gk = g_q_head[i], g_k_head[i]
        gains = jnp.stack([
            jnp.pad(gq, (0, LANES - QK_HEAD)),
            _rope_tile(jnp.concatenate([gq[QK_NOPE + ROPE_HALF:], gq[QK_NOPE:QK_NOPE + ROPE_HALF]])),
            jnp.pad(gk[:QK_NOPE], (0, LANES - QK_NOPE)),
            _rope_tile(gk[QK_NOPE:]),
            _rope_tile(jnp.concatenate([gk[QK_NOPE + ROPE_HALF:], gk[QK_NOPE:QK_NOPE + ROPE_HALF]])),
        ])
        gains = jnp.pad(gains, ((0, 3), (0, 0)))

        a_glu, q, k, v = _mixer_in(h, mod, ang8, w_in_r, g_q_lat[i].reshape(1, -1),
                                   g_kv_lat[i].reshape(1, -1), wq, wqp, wkn, wv, gains, tm,
                                   c_conv, q_lora, kv_lora, n_heads)
        att = _attention(q, k, v, tq=512)
        h = _ffn2(h, a_glu, att, mod, w_out[i].astype(BF16), ffn2_wg[i].astype(BF16),
                  ffn2_wu[i].astype(BF16), ffn2_wd[i].astype(BF16), g_final[i].reshape(1, -1),
                  conv_w[i].reshape(-1, c_conv), conv_b[i].reshape(1, -1),
                  g_conv_ln[i].reshape(1, -1), b_conv_ln[i].reshape(1, -1), tm)
    return h
```
